```python
import math
import jax, jax.numpy as jnp
from jax import lax
import numpy as np

D_MODEL = 2048
BATCH = 1
SEQ = 16384
DEPTH = 1
DEC_BATCH = 4
DEC_SEQ = 2048
PAST_LEN = 128

N_ATT_HEADS = 8
ATT_HEAD_DIM = 64
ATT_V_DIM = 2 * ATT_HEAD_DIM
ATT_QK_WIDTH = N_ATT_HEADS * 2 * ATT_HEAD_DIM
ATT_V_WIDTH = N_ATT_HEADS * ATT_V_DIM
Q_BLOCK = 128
ROPE_THETA = 10000.0

GMLP_GROUPS = 8
GMLP_GROUP_DIM = 128
GMLP_WIDTH = GMLP_GROUPS * GMLP_GROUP_DIM
GMLP_CHUNK = 128

COL_SIZES = (ATT_QK_WIDTH, ATT_QK_WIDTH, ATT_V_WIDTH, GMLP_WIDTH, GMLP_WIDTH, D_MODEL, D_MODEL)
COL_SPLITS = tuple(int(s) for s in np.cumsum(COL_SIZES)[:-1])
IN_COLS = int(sum(COL_SIZES))

N_GROUPS = 4
EXPERTS_PER_GROUP = 8
N_EXPERTS = N_GROUPS * EXPERTS_PER_GROUP
TOP_K = 2
D_EXPERT = 512

EPS = 1e-6

kernel_name = "hybrid_diffattn_gmlp_hmoe_encoder"


def rmsnorm(x, g):
    xf = x.astype(jnp.float32)
    y = xf * lax.rsqrt(jnp.mean(xf * xf, axis=-1, keepdims=True) + EPS)
    return (y * g.astype(jnp.float32)).astype(x.dtype)


def rope_tables(seq_len, dtype):
    pos = jnp.arange(seq_len, dtype=jnp.float32)
    inv = 1.0 / (ROPE_THETA ** (jnp.arange(0, ATT_HEAD_DIM, 2, dtype=jnp.float32) / ATT_HEAD_DIM))
    ang = pos[:, None] * inv[None, :]
    return jnp.cos(ang).astype(dtype), jnp.sin(ang).astype(dtype)


def apply_rope(t, cos, sin):
    c = cos[None, :, None, None, :]
    s = sin[None, :, None, None, :]
    t1, t2 = jnp.split(t, 2, axis=-1)
    return jnp.concatenate([t1 * c - t2 * s, t2 * c + t1 * s], axis=-1)


def diff_attention(q, k, v, lam, lambda_init, subln_g):
    B, S = q.shape[0], q.shape[1]
    nb = S // Q_BLOCK
    scale = ATT_HEAD_DIM ** -0.5
    qb = q.reshape(B, nb, Q_BLOCK, N_ATT_HEADS, 2, ATT_HEAD_DIM).transpose(1, 0, 2, 3, 4, 5)

    def block(qblk):
        s = jnp.einsum('bqhcd,bkhcd->bhcqk', qblk, k, preferred_element_type=jnp.float32) * scale
        p = jax.nn.softmax(s, axis=-1)
        w = p[:, :, 0] - lam * p[:, :, 1]
        return jnp.einsum('bhqk,bkhe->bqhe', w.astype(v.dtype), v)

    o = lax.map(block, qb)
    o = o.transpose(1, 0, 2, 3, 4).reshape(B, S, N_ATT_HEADS, ATT_V_DIM)
    o = rmsnorm(o, subln_g) * (1.0 - lambda_init)
    return o.reshape(B, S, ATT_V_WIDTH).astype(v.dtype)


def spatial_gating(u, gv, norm_g, w_s, b_s):
    B, S = u.shape[0], u.shape[1]
    n = S // GMLP_CHUNK
    gv = rmsnorm(gv, norm_g).reshape(B, n, GMLP_CHUNK, GMLP_GROUPS, GMLP_GROUP_DIM)
    mixed = jnp.einsum('gpq,bnqgc->bnpgc', w_s, gv) + b_s.T[:, :, None]
    return u * mixed.reshape(B, S, GMLP_WIDTH)


def hier_moe(x, w_rg, w_re, w_gate, w_up, w_down):
    B, S, D = x.shape
    xt = x.reshape(B * S, D)
    g_prob = jax.nn.softmax((xt @ w_rg).astype(jnp.float32), axis=-1)
    g_top, g_idx = lax.top_k(g_prob, 1)
    e_logits = (xt @ w_re).astype(jnp.float32).reshape(-1, N_GROUPS, EXPERTS_PER_GROUP)
    e_sel = jnp.einsum('tg,tge->te', jax.nn.one_hot(g_idx[:, 0], N_GROUPS, dtype=jnp.float32), e_logits)
    e_prob = jax.nn.softmax(e_sel, axis=-1)
    e_top, e_idx = lax.top_k(e_prob, TOP_K)
    e_top = e_top / jnp.sum(e_top, axis=-1, keepdims=True)
    weights = g_top * e_top
    expert_ids = g_idx * EXPERTS_PER_GROUP + e_idx
    combine = jnp.sum(jax.nn.one_hot(expert_ids, N_EXPERTS, dtype=jnp.float32) * weights[..., None], axis=1)
    out = jnp.zeros(xt.shape, jnp.float32)
    for e in range(N_EXPERTS):
        h = jax.nn.silu(xt @ w_gate[e]) * (xt @ w_up[e])
        out = out + combine[:, e:e + 1] * (h @ w_down[e]).astype(jnp.float32)
    return out.astype(x.dtype).reshape(B, S, D)


def encoder_layer(x, c, l, w_ada, b_ada, norm1_g, w_in, lambda_q1, lambda_k1, lambda_q2, lambda_k2,
                  subln_g, gmlp_norm_g, w_spatial, b_spatial, w_branch_att, w_branch_gmlp, w_out,
                  norm2_g, w_router_group, w_router_expert, w_gate, w_up, w_down):
    B, S, _ = x.shape
    lambda_init = 0.8 - 0.6 * math.exp(-0.3 * l)
    mod = (jax.nn.silu(c) @ w_ada[l] + b_ada[l])[:, None, :]
    sh1, sc1, g1, sh2, sc2, g2 = jnp.split(mod, 6, axis=-1)

    h = rmsnorm(x, norm1_g[l]) * (1.0 + sc1) + sh1
    z = h @ w_in[l]
    q, k, v, gu, gv, ga, gb = jnp.split(z, COL_SPLITS, axis=-1)
    cos, sin = rope_tables(S, x.dtype)
    q = apply_rope(q.reshape(B, S, N_ATT_HEADS, 2, ATT_HEAD_DIM), cos, sin)
    k = apply_rope(k.reshape(B, S, N_ATT_HEADS, 2, ATT_HEAD_DIM), cos, sin)
    v = v.reshape(B, S, N_ATT_HEADS, ATT_V_DIM)
    lam = (jnp.exp(jnp.sum(lambda_q1[l].astype(jnp.float32) * lambda_k1[l].astype(jnp.float32)))
           - jnp.exp(jnp.sum(lambda_q2[l].astype(jnp.float32) * lambda_k2[l].astype(jnp.float32)))
           + lambda_init)
    a = diff_attention(q, k, v, lam, lambda_init, subln_g[l])
    m = spatial_gating(jax.nn.gelu(gu), jax.nn.gelu(gv), gmlp_norm_g[l], w_spatial[l], b_spatial[l])
    merged = jax.nn.sigmoid(ga) * (a @ w_branch_att[l]) + jax.nn.sigmoid(gb) * (m @ w_branch_gmlp[l])
    x = x + g1 * (merged @ w_out[l])

    h2 = rmsnorm(x, norm2_g[l]) * (1.0 + sc2) + sh2
    x = x + g2 * hier_moe(h2, w_router_group[l], w_router_expert[l], w_gate[l], w_up[l], w_down[l])
    return x


def setup_inputs(seed: int = 0) -> dict:
    key = jax.random.key(seed)
    ks = jax.random.split(key, 32)
    f32 = jnp.float32
    D = D_MODEL

    def nrm(k, shape, scale):
        return jax.random.normal(k, shape, f32) * scale

    return {
        "x_prompt": nrm(ks[0], (BATCH, SEQ, D), 1.0),
        "x_sample": nrm(ks[1], (DEC_BATCH, DEC_SEQ, D), 1.0),
        "c_prompt": nrm(ks[2], (BATCH, D), 1.0),
        "c_sample": nrm(ks[3], (DEC_BATCH, D), 1.0),
        "w_ada": nrm(ks[4], (DEPTH, D, 6 * D), D ** -0.5),
        "b_ada": nrm(ks[5], (DEPTH, 6 * D), 0.02),
        "norm1_g": 1.0 + nrm(ks[6], (DEPTH, D), 0.05),
        "w_in": nrm(ks[7], (DEPTH, D, IN_COLS), D ** -0.5),
        "lambda_q1": nrm(ks[8], (DEPTH, ATT_HEAD_DIM), 0.1),
        "lambda_k1": nrm(ks[9], (DEPTH, ATT_HEAD_DIM), 0.1),
        "lambda_q2": nrm(ks[10], (DEPTH, ATT_HEAD_DIM), 0.1),
        "lambda_k2": nrm(ks[11], (DEPTH, ATT_HEAD_DIM), 0.1),
        "subln_g": 1.0 + nrm(ks[12], (DEPTH, ATT_V_DIM), 0.05),
        "gmlp_norm_g": 1.0 + nrm(ks[13], (DEPTH, GMLP_WIDTH), 0.05),
        "w_spatial": nrm(ks[14], (DEPTH, GMLP_GROUPS, GMLP_CHUNK, GMLP_CHUNK), GMLP_CHUNK ** -0.5),
        "b_spatial": 1.0 + nrm(ks[15], (DEPTH, GMLP_GROUPS, GMLP_CHUNK), 0.05),
        "w_branch_att": nrm(ks[16], (DEPTH, ATT_V_WIDTH, D), ATT_V_WIDTH ** -0.5),
        "w_branch_gmlp": nrm(ks[17], (DEPTH, GMLP_WIDTH, D), GMLP_WIDTH ** -0.5),
        "w_out": nrm(ks[18], (DEPTH, D, D), D ** -0.5),
        "norm2_g": 1.0 + nrm(ks[19], (DEPTH, D), 0.05),
        "w_router_group": nrm(ks[20], (DEPTH, D, N_GROUPS), D ** -0.5),
        "w_router_expert": nrm(ks[21], (DEPTH, D, N_EXPERTS), D ** -0.5),
        "w_gate": nrm(ks[22], (DEPTH, N_EXPERTS, D, D_EXPERT), D ** -0.5),
        "w_up": nrm(ks[23], (DEPTH, N_EXPERTS, D, D_EXPERT), D ** -0.5),
        "w_down": nrm(ks[24], (DEPTH, N_EXPERTS, D_EXPERT, D), D_EXPERT ** -0.5),
        "final_norm_g": 1.0 + nrm(ks[25], (D,), 0.05),
    }


def reference(x_prompt, x_sample, c_prompt, c_sample, w_ada, b_ada, norm1_g, w_in,
              lambda_q1, lambda_k1, lambda_q2, lambda_k2, subln_g, gmlp_norm_g, w_spatial, b_spatial,
              w_branch_att, w_branch_gmlp, w_out, norm2_g, w_router_group, w_router_expert,
              w_gate, w_up, w_down, final_norm_g):
    def trunk(x, c):
        for l in range(DEPTH):
            x = encoder_layer(x, c, l, w_ada, b_ada, norm1_g, w_in, lambda_q1, lambda_k1, lambda_q2,
                              lambda_k2, subln_g, gmlp_norm_g, w_spatial, b_spatial, w_branch_att,
                              w_branch_gmlp, w_out, norm2_g, w_router_group, w_router_expert,
                              w_gate, w_up, w_down)
        return rmsnorm(x, final_norm_g)

    y_prompt = trunk(x_prompt, c_prompt)
    y_sample = trunk(x_sample, c_sample)
    return (y_prompt, y_sample)
```

```python
import functools
import math

import jax
import jax.numpy as jnp
from jax import lax
from jax.experimental import pallas as pl
from jax.experimental.pallas import tpu as pltpu

D_MODEL = 2048
N_HEADS = 8
HEAD_DIM = 64
V_DIM = 2 * HEAD_DIM
QK_WIDTH = N_HEADS * 2 * HEAD_DIM
V_WIDTH = N_HEADS * V_DIM
ROPE_THETA = 10000.0
GMLP_GROUPS = 8
GMLP_GROUP_DIM = 128
GMLP_WIDTH = GMLP_GROUPS * GMLP_GROUP_DIM
GMLP_CHUNK = 128
IN_COLS = 3 * 1024 + 2 * 1024 + 2 * D_MODEL
N_GROUPS = 4
EXPERTS_PER_GROUP = 8
N_EXPERTS = N_GROUPS * EXPERTS_PER_GROUP
D_EXPERT = 512
EPS = 1e-6
LAMBDA_INIT = 0.8 - 0.6 * math.exp(-0.3 * 0)

LANES = 128
COL_TILE = 1024
N_COL_TILES = IN_COLS // COL_TILE
VMEM_LIMIT = 56 * 1024 * 1024

TM_IN = 512
TM_MIX = 512
TM_OUT = 256
TQ = 512
TK = 512
TM_EXP = 256
TM_DISP = 512
TM_COMB = 256


def _cparams(sem):
    return pltpu.CompilerParams(dimension_semantics=sem, vmem_limit_bytes=VMEM_LIMIT)


def _rms(x, g):
    return x * lax.rsqrt(jnp.mean(x * x, axis=-1, keepdims=True) + EPS) * g


def _ada_kernel(c_ref, w_ref, b_ref, o_ref):
    c = c_ref[...]
    s = (c / (1.0 + jnp.exp(-c))).astype(jnp.bfloat16)
    o_ref[...] = jnp.dot(s, w_ref[...].astype(jnp.bfloat16),
                         preferred_element_type=jnp.float32) + b_ref[...]


def _ada(c8, w_ada, b_ada):
    n = w_ada.shape[1]
    tn = 1024
    return pl.pallas_call(
        _ada_kernel,
        grid=(n // tn,),
        in_specs=[pl.BlockSpec((8, D_MODEL), lambda j: (0, 0)),
                  pl.BlockSpec((D_MODEL, tn), lambda j: (0, j)),
                  pl.BlockSpec((1, tn), lambda j: (0, j))],
        out_specs=pl.BlockSpec((8, tn), lambda j: (0, j)),
        out_shape=jax.ShapeDtypeStruct((8, n), jnp.float32),
        compiler_params=_cparams(("arbitrary",)),
        name="ada",
    )(c8, w_ada, b_ada)


class _Rows:
    def __init__(self, n_prompt, n_sample_seq, sample_len):
        self.np = n_prompt
        self.ns = n_sample_seq
        self.ls = sample_len
        self.t = n_prompt + n_sample_seq * sample_len

    def seq_of_tile(self, i, tm):
        npt = self.np // tm
        return jnp.where(i < npt, 0, 1 + (i - npt) // (self.ls // tm))

    def pos_tile(self, i, tm):
        npt = self.np // tm
        return jnp.where(i < npt, i, (i - npt) % (self.ls // tm))

    def prompt_tile(self, i, tm):
        return jnp.minimum(i, self.np // tm - 1)

    def sample_tile(self, i, tm):
        return jnp.maximum(i - self.np // tm, 0)


def _inproj_kernel(rows, xp_ref, xs_ref, mod_ref, g1_ref, w_ref, cos_ref, sin_ref, gn_ref,
                   z_ref, h_ref, acc_ref):
    i = pl.program_id(0)
    j = pl.program_id(1)
    tm = h_ref.shape[0]

    def fill_h(x_ref):
        m = mod_ref[0]
        h = _rms(x_ref[...], g1_ref[...]) * (1.0 + m[1:2, :]) + m[0:1, :]
        h_ref[...] = h.astype(jnp.bfloat16)

    @pl.when((j == 0) & (i < rows.np // tm))
    def _():
        fill_h(xp_ref)

    @pl.when((j == 0) & (i >= rows.np // tm))
    def _():
        fill_h(xs_ref)

    acc_ref[...] = jnp.dot(h_ref[...], w_ref[...], preferred_element_type=jnp.float32)

    def rope(scale):
        c = cos_ref[...]
        s = sin_ref[...]
        lane = lax.broadcasted_iota(jnp.int32, (tm, LANES), 1)
        first_half = (lane % HEAD_DIM) < (HEAD_DIM // 2)
        for b in range(COL_TILE // LANES):
            zb = acc_ref[:, b * LANES:(b + 1) * LANES]
            up = pltpu.roll(zb, LANES - HEAD_DIM // 2, 1)
            dn = pltpu.roll(zb, HEAD_DIM // 2, 1)
            r = (zb * c + jnp.where(first_half, up, dn) * s) * scale
            z_ref[:, b * LANES:(b + 1) * LANES] = r.astype(jnp.bfloat16)

    @pl.when(j == 0)
    def _():
        rope(HEAD_DIM ** -0.5 * math.log2(math.e))

    @pl.when(j == 1)
    def _():
        rope(1.0)

    @pl.when(j == 2)
    def _():
        z_ref[...] = acc_ref[...].astype(jnp.bfloat16)

    @pl.when(j == 3)
    def _():
        z_ref[...] = jax.nn.gelu(acc_ref[...], approximate=True).astype(jnp.bfloat16)

    @pl.when(j == 4)
    def _():
        z_ref[...] = _rms(jax.nn.gelu(acc_ref[...], approximate=True), gn_ref[...]).astype(jnp.bfloat16)

    @pl.when(j >= 5)
    def _():
        z_ref[...] = (1.0 / (1.0 + jnp.exp(-acc_ref[...]))).astype(jnp.bfloat16)


def _inproj(rows, xp, xs, mod3, g1, w_in_bf, cos_t, sin_t, gn):
    tm = TM_IN
    return pl.pallas_call(
        functools.partial(_inproj_kernel, rows),
        grid=(rows.t // tm, N_COL_TILES),
        in_specs=[
            pl.BlockSpec((tm, D_MODEL), lambda i, j: (rows.prompt_tile(i, tm), 0)),
            pl.BlockSpec((tm, D_MODEL), lambda i, j: (rows.sample_tile(i, tm), 0)),
            pl.BlockSpec((1, 6, D_MODEL), lambda i, j: (rows.seq_of_tile(i, tm), 0, 0)),
            pl.BlockSpec((1, D_MODEL), lambda i, j: (0, 0)),
            pl.BlockSpec((D_MODEL, COL_TILE), lambda i, j: (0, j)),
            pl.BlockSpec((tm, LANES), lambda i, j: (rows.pos_tile(i, tm), 0)),
            pl.BlockSpec((tm, LANES), lambda i, j: (rows.pos_tile(i, tm), 0)),
            pl.BlockSpec((1, COL_TILE), lambda i, j: (0, 0)),
        ],
        out_specs=pl.BlockSpec((tm, COL_TILE), lambda i, j: (i, j)),
        out_shape=jax.ShapeDtypeStruct((rows.t, IN_COLS), jnp.bfloat16),
        scratch_shapes=[pltpu.VMEM((tm, D_MODEL), jnp.bfloat16),
                        pltpu.VMEM((tm, COL_TILE), jnp.float32)],
        compiler_params=_cparams(("arbitrary", "arbitrary")),
        name="inproj",
    )(xp, xs, mod3, g1, w_in_bf, cos_t, sin_t, gn)


def _attn_kernel(n_k, q_ref, k_ref, v_ref, lam_ref, sg_ref, o_ref, qq_ref, acc_ref, m_ref, l_ref):
    tq = q_ref.shape[0]
    q = q_ref[...]
    lane = lax.broadcasted_iota(jnp.int32, q.shape, 1)
    zero = jnp.zeros_like(q)
    qq_ref[0:tq, :] = jnp.where(lane < HEAD_DIM, q, zero)
    qq_ref[tq:2 * tq, :] = jnp.where(lane >= HEAD_DIM, q, zero)
    m_ref[...] = jnp.full(m_ref.shape, -jnp.inf, jnp.float32)
    l_ref[...] = jnp.zeros(l_ref.shape, jnp.float32)
    acc_ref[...] = jnp.zeros(acc_ref.shape, jnp.float32)

    def body(ik, carry):
        off = pl.multiple_of(ik * TK, TK)
        k = k_ref[pl.ds(off, TK), :]
        v = v_ref[pl.ds(off, TK), :]
        s = lax.dot_general(k, qq_ref[...], (((1,), (1,)), ((), ())),
                            preferred_element_type=jnp.float32)
        m_old = m_ref[...]
        m_new = jnp.maximum(m_old, jnp.max(s, axis=0, keepdims=True))
        p = jnp.exp2(s - m_new)
        alpha = jnp.exp2(m_old - m_new)
        l_ref[...] = alpha * l_ref[...] + jnp.sum(p, axis=0, keepdims=True)
        m_ref[...] = m_new
        pv = lax.dot_general(v, p.astype(jnp.bfloat16), (((0,), (0,)), ((), ())),
                             preferred_element_type=jnp.float32)
        acc_ref[...] = alpha * acc_ref[...] + pv
        return carry

    lax.fori_loop(0, n_k, body, 0)

    lp = lam_ref[...]
    lam = (jnp.exp(jnp.sum(lp[0:1, :] * lp[1:2, :], axis=1, keepdims=True))
           - jnp.exp(jnp.sum(lp[2:3, :] * lp[3:4, :], axis=1, keepdims=True)) + LAMBDA_INIT)
    acc = acc_ref[...]
    l = l_ref[...]
    o_t = acc[:, 0:tq] / l[:, 0:tq] - lam * (acc[:, tq:2 * tq] / l[:, tq:2 * tq])
    o = o_t.T
    o_ref[...] = (_rms(o, sg_ref[...]) * (1.0 - LAMBDA_INIT)).astype(jnp.bfloat16)


def _attn(z, lam_p, subln_g, n_seq, seq_len, row0):
    tq = min(TQ, seq_len)
    nq = seq_len // tq
    qb0 = row0 // tq
    sb0 = row0 // seq_len
    kern = functools.partial(_attn_kernel, seq_len // TK)
    return pl.pallas_call(
        kern,
        grid=(n_seq, N_HEADS, nq),
        in_specs=[
            pl.BlockSpec((tq, LANES), lambda b, h, iq: (qb0 + b * nq + iq, h)),
            pl.BlockSpec((seq_len, LANES), lambda b, h, iq: (sb0 + b, N_HEADS + h)),
            pl.BlockSpec((seq_len, LANES), lambda b, h, iq: (sb0 + b, 2 * N_HEADS + h)),
            pl.BlockSpec((4, HEAD_DIM), lambda b, h, iq: (0, 0)),
            pl.BlockSpec((1, V_DIM), lambda b, h, iq: (0, 0)),
        ],
        out_specs=pl.BlockSpec((tq, LANES), lambda b, h, iq: (b * nq + iq, h)),
        out_shape=jax.ShapeDtypeStruct((n_seq * seq_len, V_WIDTH), jnp.bfloat16),
        scratch_shapes=[pltpu.VMEM((2 * tq, LANES), jnp.bfloat16),
                        pltpu.VMEM((V_DIM, 2 * tq), jnp.float32),
                        pltpu.VMEM((1, 2 * tq), jnp.float32),
                        pltpu.VMEM((1, 2 * tq), jnp.float32)],
        compiler_params=_cparams(("arbitrary", "arbitrary", "arbitrary")),
        name="attn",
    )(z, z, z, lam_p, subln_g)


def _mix_kernel(ap_ref, as_ref, gu_ref, gv_ref, sa0_ref, sa1_ref, sb0_ref, sb1_ref,
                ws_ref, bs_ref, wba_ref, wbg_ref, o_ref, m_ref, *, n_prompt_tiles):
    i = pl.program_id(0)
    tm = gu_ref.shape[0]
    bias = bs_ref[...]
    for c in range(tm // GMLP_CHUNK):
        r0 = c * GMLP_CHUNK
        for g in range(GMLP_GROUPS):
            c0 = g * GMLP_GROUP_DIM
            mixed = jnp.dot(ws_ref[g], gv_ref[r0:r0 + GMLP_CHUNK, c0:c0 + GMLP_GROUP_DIM],
                            preferred_element_type=jnp.float32) + bias[:, c0:c0 + GMLP_GROUP_DIM]
            gu = gu_ref[r0:r0 + GMLP_CHUNK, c0:c0 + GMLP_GROUP_DIM].astype(jnp.float32)
            m_ref[r0:r0 + GMLP_CHUNK, c0:c0 + GMLP_GROUP_DIM] = (gu * mixed).astype(jnp.bfloat16)

    def finish(a_ref):
        a = a_ref[...]
        m = m_ref[...]
        for nb, (sa_ref, sb_ref) in enumerate(((sa0_ref, sb0_ref), (sa1_ref, sb1_ref))):
            cs = slice(nb * COL_TILE, (nb + 1) * COL_TILE)
            pa = jnp.dot(a, wba_ref[:, cs], preferred_element_type=jnp.float32)
            pb = jnp.dot(m, wbg_ref[:, cs], preferred_element_type=jnp.float32)
            merged = sa_ref[...].astype(jnp.float32) * pa + sb_ref[...].astype(jnp.float32) * pb
            o_ref[:, cs] = merged.astype(jnp.bfloat16)

    @pl.when(i < n_prompt_tiles)
    def _():
        finish(ap_ref)

    @pl.when(i >= n_prompt_tiles)
    def _():
        finish(as_ref)


def _mix(rows, a_p, a_s, z, ws_bf, bias_tile, wba_bf, wbg_bf):
    tm = TM_MIX
    zcol = lambda cb: pl.BlockSpec((tm, COL_TILE), lambda i: (i, cb))
    return pl.pallas_call(
        functools.partial(_mix_kernel, n_prompt_tiles=rows.np // tm),
        grid=(rows.t // tm,),
        in_specs=[
            pl.BlockSpec((tm, V_WIDTH), lambda i: (rows.prompt_tile(i, tm), 0)),
            pl.BlockSpec((tm, V_WIDTH), lambda i: (rows.sample_tile(i, tm), 0)),
            zcol(3), zcol(4), zcol(5), zcol(6), zcol(7), zcol(8),
            pl.BlockSpec((GMLP_GROUPS, GMLP_CHUNK, GMLP_CHUNK), lambda i: (0, 0, 0)),
            pl.BlockSpec((GMLP_CHUNK, GMLP_WIDTH), lambda i: (0, 0)),
            pl.BlockSpec((V_WIDTH, D_MODEL), lambda i: (0, 0)),
            pl.BlockSpec((GMLP_WIDTH, D_MODEL), lambda i: (0, 0)),
        ],
        out_specs=pl.BlockSpec((tm, D_MODEL), lambda i: (i, 0)),
        out_shape=jax.ShapeDtypeStruct((rows.t, D_MODEL), jnp.bfloat16),
        scratch_shapes=[pltpu.VMEM((tm, GMLP_WIDTH), jnp.bfloat16)],
        compiler_params=_cparams(("arbitrary",)),
        name="mix",
    )(a_p, a_s, z, z, z, z, z, z, ws_bf, bias_tile, wba_bf, wbg_bf)


def _outproj_kernel(rows, mg_ref, xp_ref, xs_ref, mod_ref, g2_ref, wo_ref, wrh_ref, wrl_ref, tri_ref,
                    x1_ref, h2_ref, ri_ref, rf_ref, cnt_ref, base_ref):
    i = pl.program_id(0)
    tm = mg_ref.shape[0]

    @pl.when(i == 0)
    def _():
        base_ref[...] = jnp.zeros(base_ref.shape, jnp.float32)

    m = mod_ref[0]
    y = jnp.dot(mg_ref[...], wo_ref[...], preferred_element_type=jnp.float32)

    def residual(x_ref):
        x1 = x_ref[...] + m[2:3, :] * y
        x1_ref[...] = x1
        h2_ref[...] = _rms(x1, g2_ref[...]) * (1.0 + m[4:5, :]) + m[3:4, :]

    @pl.when(i < rows.np // tm)
    def _():
        residual(xp_ref)

    @pl.when(i >= rows.np // tm)
    def _():
        residual(xs_ref)

    h2 = h2_ref[...]
    h_hi = h2.astype(jnp.bfloat16)
    h_lo = (h2 - h_hi.astype(jnp.float32)).astype(jnp.bfloat16)
    nt = (((1,), (1,)), ((), ()))
    lg = (lax.dot_general(wrh_ref[...], h_hi, nt, preferred_element_type=jnp.float32)
          + lax.dot_general(wrh_ref[...], h_lo, nt, preferred_element_type=jnp.float32)
          + lax.dot_general(wrl_ref[...], h_hi, nt, preferred_element_type=jnp.float32))

    row8 = lax.broadcasted_iota(jnp.int32, (8, tm), 0)
    neg = jnp.float32(-jnp.inf)
    gl = jnp.where(row8 < N_GROUPS, lg[N_EXPERTS:N_EXPERTS + 8, :], neg)
    gmax = jnp.max(gl, axis=0, keepdims=True)
    g_top = 1.0 / jnp.sum(jnp.exp(gl - gmax), axis=0, keepdims=True)
    g_idx = jnp.min(jnp.where(gl == gmax, row8, 8), axis=0, keepdims=True)
    e_sel = jnp.zeros((8, tm), jnp.float32)
    for g in range(N_GROUPS):
        e_sel = jnp.where(g_idx == g, lg[g * EXPERTS_PER_GROUP:(g + 1) * EXPERTS_PER_GROUP, :], e_sel)
    e1 = jnp.max(e_sel, axis=0, keepdims=True)
    i1 = jnp.min(jnp.where(e_sel == e1, row8, 8), axis=0, keepdims=True)
    e_rest = jnp.where(row8 == i1, neg, e_sel)
    e2 = jnp.max(e_rest, axis=0, keepdims=True)
    i2 = jnp.min(jnp.where(e_rest == e2, row8, 8), axis=0, keepdims=True)
    t = jnp.exp(e2 - e1)
    w1 = g_top / (1.0 + t)
    w2 = g_top * t / (1.0 + t)
    id1 = g_idx * EXPERTS_PER_GROUP + i1
    id2 = g_idx * EXPERTS_PER_GROUP + i2

    row32 = lax.broadcasted_iota(jnp.int32, (N_EXPERTS, tm), 0)
    oh1 = (row32 == id1).astype(jnp.float32)
    oh2 = (row32 == id2).astype(jnp.float32)
    both = oh1 + oh2
    before = jnp.dot(both.astype(jnp.bfloat16), tri_ref[...], preferred_element_type=jnp.float32)
    base = base_ref[...]
    tot = before + jnp.concatenate([base] * (tm // LANES), axis=1)
    rank1 = jnp.sum(oh1 * tot, axis=0, keepdims=True)
    rank2 = jnp.sum(oh2 * tot, axis=0, keepdims=True)
    new_base = base + jnp.sum(both, axis=1, keepdims=True)
    base_ref[...] = new_base
    cnt_ref[...] = new_base.astype(jnp.int32)

    zi = jnp.zeros((4, tm), jnp.int32)
    ri_ref[...] = jnp.concatenate([id1, id2, rank1.astype(jnp.int32), rank2.astype(jnp.int32), zi], axis=0)
    rf_ref[...] = jnp.concatenate([w1, w2, jnp.zeros((6, tm), jnp.float32)], axis=0)


def _outproj(rows, merged, xp, xs, mod3, g2, wo_bf, wr_hi, wr_lo, tri):
    tm = TM_OUT
    t = rows.t
    return pl.pallas_call(
        functools.partial(_outproj_kernel, rows),
        grid=(t // tm,),
        in_specs=[
            pl.BlockSpec((tm, D_MODEL), lambda i: (i, 0)),
            pl.BlockSpec((tm, D_MODEL), lambda i: (rows.prompt_tile(i, tm), 0)),
            pl.BlockSpec((tm, D_MODEL), lambda i: (rows.sample_tile(i, tm), 0)),
            pl.BlockSpec((1, 6, D_MODEL), lambda i: (rows.seq_of_tile(i, tm), 0, 0)),
            pl.BlockSpec((1, D_MODEL), lambda i: (0, 0)),
            pl.BlockSpec((D_MODEL, D_MODEL), lambda i: (0, 0)),
            pl.BlockSpec((N_EXPERTS + 8, D_MODEL), lambda i: (0, 0)),
            pl.BlockSpec((N_EXPERTS + 8, D_MODEL), lambda i: (0, 0)),
            pl.BlockSpec((tm, tm), lambda i: (0, 0)),
        ],
        out_specs=[
            pl.BlockSpec((tm, D_MODEL), lambda i: (i, 0)),
            pl.BlockSpec((tm, D_MODEL), lambda i: (i, 0)),
            pl.BlockSpec((8, tm), lambda i: (0, i)),
            pl.BlockSpec((8, tm), lambda i: (0, i)),
            pl.BlockSpec((N_EXPERTS, LANES), lambda i: (0, 0)),
        ],
        out_shape=[
            jax.ShapeDtypeStruct((t, D_MODEL), jnp.float32),
            jax.ShapeDtypeStruct((t, D_MODEL), jnp.float32),
            jax.ShapeDtypeStruct((8, t), jnp.int32),
            jax.ShapeDtypeStruct((8, t), jnp.float32),
            jax.ShapeDtypeStruct((N_EXPERTS, LANES), jnp.int32),
        ],
        scratch_shapes=[pltpu.VMEM((N_EXPERTS, LANES), jnp.float32)],
        compiler_params=_cparams(("arbitrary",)),
        name="outproj",
    )(merged, xp, xs, mod3, g2, wo_bf, wr_hi, wr_lo, tri)


def _dispatch_kernel(pos_ref, h2_ref, init_ref, xs_ref, sem):
    del init_ref
    i = pl.program_id(0)
    t_total = h2_ref.shape[0]
    t0 = i * TM_DISP

    def row_copy(tok, dst):
        return pltpu.make_async_copy(h2_ref.at[pl.ds(tok, 1), :], xs_ref.at[pl.ds(dst, 1), :], sem)

    def issue(r, carry):
        tok = t0 + r
        row_copy(tok, pos_ref[tok]).start()
        row_copy(tok, pos_ref[t_total + tok]).start()
        return carry

    lax.fori_loop(0, TM_DISP, issue, 0, unroll=8)

    def drain(r, carry):
        row_copy(0, 0).wait()
        row_copy(0, 0).wait()
        return carry

    lax.fori_loop(0, TM_DISP, drain, 0, unroll=8)


def _dispatch(pos_flat, h2, n_rows):
    t = h2.shape[0]
    init = jnp.zeros((n_rows, D_MODEL), jnp.float32)
    return pl.pallas_call(
        _dispatch_kernel,
        grid_spec=pltpu.PrefetchScalarGridSpec(
            num_scalar_prefetch=1,
            grid=(t // TM_DISP,),
            in_specs=[pl.BlockSpec(memory_space=pl.ANY), pl.BlockSpec(memory_space=pl.ANY)],
            out_specs=pl.BlockSpec(memory_space=pl.ANY),
            scratch_shapes=[pltpu.SemaphoreType.DMA],
        ),
        out_shape=jax.ShapeDtypeStruct((n_rows, D_MODEL), jnp.float32),
        input_output_aliases={2: 0},
        compiler_params=pltpu.CompilerParams(dimension_semantics=("arbitrary",)),
        name="dispatch",
    )(pos_flat, h2, init)


def _experts_kernel(meta_ref, x_ref, wg_ref, wu_ref, wd_ref, y_ref):
    j = pl.program_id(0)

    @pl.when(j < meta_ref[0])
    def _():
        x = x_ref[...].astype(jnp.bfloat16)
        g = jnp.dot(x, wg_ref[0], preferred_element_type=jnp.float32)
        u = jnp.dot(x, wu_ref[0], preferred_element_type=jnp.float32)
        hm = (g / (1.0 + jnp.exp(-g)) * u).astype(jnp.bfloat16)
        y_ref[...] = jnp.dot(hm, wd_ref[0], preferred_element_type=jnp.float32)

    @pl.when(j >= meta_ref[0])
    def _():
        y_ref[...] = jnp.zeros(y_ref.shape, jnp.float32)


def _experts(meta, xs, wg_bf, wu_bf, wd_bf):
    n_rows = xs.shape[0]
    n_tiles = n_rows // TM_EXP
    row = lambda j, meta: (jnp.minimum(j, meta[0] - 1), 0)
    wsel = lambda j, meta: (meta[1 + j], 0, 0)
    return pl.pallas_call(
        _experts_kernel,
        grid_spec=pltpu.PrefetchScalarGridSpec(
            num_scalar_prefetch=1,
            grid=(n_tiles,),
            in_specs=[pl.BlockSpec((TM_EXP, D_MODEL), row),
                      pl.BlockSpec((1, D_MODEL, D_EXPERT), wsel),
                      pl.BlockSpec((1, D_MODEL, D_EXPERT), wsel),
                      pl.BlockSpec((1, D_EXPERT, D_MODEL), wsel)],
            out_specs=pl.BlockSpec((TM_EXP, D_MODEL), lambda j, meta: (j, 0)),
        ),
        out_shape=jax.ShapeDtypeStruct((n_rows, D_MODEL), jnp.float32),
        compiler_params=_cparams(("arbitrary",)),
        name="experts",
    )(meta, xs, wg_bf, wu_bf, wd_bf)


def _combine_kernel(rows, pos_ref, ys_ref, x1_ref, rf_ref, mod_ref, gf_ref, op_ref, os_ref, buf_ref, sem):
    i = pl.program_id(0)
    tm = x1_ref.shape[0]
    t0 = i * tm

    def row_copy(src, slot, r):
        return pltpu.make_async_copy(ys_ref.at[pl.ds(src, 1), :], buf_ref.at[slot, pl.ds(r, 1), :], sem)

    def issue(r, carry):
        row_copy(pos_ref[t0 + r], 0, r).start()
        row_copy(pos_ref[rows.t + t0 + r], 1, r).start()
        return carry

    lax.fori_loop(0, tm, issue, 0, unroll=8)

    def drain(r, carry):
        row_copy(0, 0, 0).wait()
        row_copy(0, 1, 0).wait()
        return carry

    lax.fori_loop(0, tm, drain, 0, unroll=8)

    w = rf_ref[...].T
    moe = w[:, 0:1] * buf_ref[0] + w[:, 1:2] * buf_ref[1]
    x2 = x1_ref[...] + mod_ref[0][5:6, :] * moe
    out = _rms(x2, gf_ref[...])

    @pl.when(i < rows.np // tm)
    def _():
        op_ref[...] = out

    @pl.when(i >= rows.np // tm)
    def _():
        os_ref[...] = out


def _combine(rows, pos_flat, ys, x1, rf, mod3, gf):
    tm = TM_COMB
    return pl.pallas_call(
        functools.partial(_combine_kernel, rows),
        grid_spec=pltpu.PrefetchScalarGridSpec(
            num_scalar_prefetch=1,
            grid=(rows.t // tm,),
            in_specs=[
                pl.BlockSpec(memory_space=pl.ANY),
                pl.BlockSpec((tm, D_MODEL), lambda i, pos: (i, 0)),
                pl.BlockSpec((8, tm), lambda i, pos: (0, i)),
                pl.BlockSpec((1, 6, D_MODEL), lambda i, pos: (rows.seq_of_tile(i, tm), 0, 0)),
                pl.BlockSpec((1, D_MODEL), lambda i, pos: (0, 0)),
            ],
            out_specs=[
                pl.BlockSpec((tm, D_MODEL), lambda i, pos: (rows.prompt_tile(i, tm), 0)),
                pl.BlockSpec((tm, D_MODEL), lambda i, pos: (rows.sample_tile(i, tm), 0)),
            ],
            scratch_shapes=[pltpu.VMEM((2, tm, D_MODEL), jnp.float32), pltpu.SemaphoreType.DMA],
        ),
        out_shape=[jax.ShapeDtypeStruct((rows.np, D_MODEL), jnp.float32),
                   jax.ShapeDtypeStruct((rows.t - rows.np, D_MODEL), jnp.float32)],
        compiler_params=_cparams(("arbitrary",)),
        name="combine",
    )(pos_flat, ys, x1, rf, mod3, gf)


def _rope_tables(seq_len):
    pos = jnp.arange(seq_len, dtype=jnp.float32)
    inv = 1.0 / (ROPE_THETA ** (jnp.arange(0, HEAD_DIM, 2, dtype=jnp.float32) / HEAD_DIM))
    ang = pos[:, None] * inv[None, :]
    cos, sin = jnp.cos(ang), jnp.sin(ang)
    cos_t = jnp.tile(cos, (1, LANES // (HEAD_DIM // 2)))
    sin_t = jnp.tile(jnp.concatenate([-sin, sin], axis=1), (1, LANES // HEAD_DIM))
    return cos_t, sin_t


def _routing_tables(ri, cnt, n_tiles_max):
    counts = cnt[:, 0]
    padded = (counts + TM_EXP - 1) // TM_EXP * TM_EXP
    seg_end = jnp.cumsum(padded)
    seg_start = seg_end - padded
    pos1 = seg_start[ri[0]] + ri[2]
    pos2 = seg_start[ri[1]] + ri[3]
    n_active = seg_end[-1] // TM_EXP
    tile_row = jnp.minimum(jnp.arange(n_tiles_max, dtype=jnp.int32), n_active - 1) * TM_EXP
    tile_expert = jnp.sum((tile_row[:, None] >= seg_end[None, :]).astype(jnp.int32), axis=1)
    meta = jnp.concatenate([n_active[None].astype(jnp.int32), tile_expert.astype(jnp.int32)])
    return jnp.concatenate([pos1, pos2]).astype(jnp.int32), meta


def kernel(x_prompt, x_sample, c_prompt, c_sample, w_ada, b_ada, norm1_g, w_in, lambda_q1, lambda_k1,
           lambda_q2, lambda_k2, subln_g, gmlp_norm_g, w_spatial, b_spatial, w_branch_att,
           w_branch_gmlp, w_out, norm2_g, w_router_group, w_router_expert, w_gate, w_up, w_down,
           final_norm_g):
    bf = jnp.bfloat16
    bp, sp, d = x_prompt.shape
    bs, ss, _ = x_sample.shape
    assert bp == 1 and d == D_MODEL
    rows = _Rows(bp * sp, bs, ss)
    xp = x_prompt.reshape(bp * sp, d)
    xs = x_sample.reshape(bs * ss, d)

    c8 = jnp.concatenate([c_prompt, c_sample, jnp.zeros((8 - bp - bs, d), jnp.float32)], axis=0)
    mod3 = _ada(c8, w_ada[0], b_ada).reshape(8, 6, d)

    cos_t, sin_t = _rope_tables(max(sp, ss))
    z = _inproj(rows, xp, xs, mod3, norm1_g, w_in[0].astype(bf), cos_t, sin_t, gmlp_norm_g)

    lam_p = jnp.concatenate([lambda_q1, lambda_k1, lambda_q2, lambda_k2], axis=0)
    a_p = _attn(z, lam_p, subln_g, bp, sp, 0)
    a_s = _attn(z, lam_p, subln_g, bs, ss, bp * sp)

    bias_tile = jnp.repeat(b_spatial[0].T, GMLP_GROUP_DIM, axis=1)
    merged = _mix(rows, a_p, a_s, z, w_spatial[0].astype(bf), bias_tile,
                  w_branch_att[0].astype(bf), w_branch_gmlp[0].astype(bf))

    wr_t = jnp.concatenate([w_router_expert[0].T, w_router_group[0].T,
                            jnp.zeros((8 - N_GROUPS, d), jnp.float32)], axis=0)
    wr_hi = wr_t.astype(bf)
    wr_lo = (wr_t - wr_hi.astype(jnp.float32)).astype(bf)
    tri = jnp.triu(jnp.ones((TM_OUT, TM_OUT), bf), k=1)
    x1, h2, ri, rf, cnt = _outproj(rows, merged, xp, xs, mod3, norm2_g, w_out[0].astype(bf),
                                   wr_hi, wr_lo, tri)

    n_rows = (2 * rows.t // TM_EXP + N_EXPERTS) * TM_EXP
    pos_flat, meta = _routing_tables(ri, cnt, n_rows // TM_EXP)
    xs_sorted = _dispatch(pos_flat, h2, n_rows)
    ys = _experts(meta, xs_sorted, w_gate[0].astype(bf), w_up[0].astype(bf), w_down[0].astype(bf))
    y_p, y_s = _combine(rows, pos_flat, ys, x1, rf, mod3, final_norm_g.reshape(1, d))
    return y_p.reshape(bp, sp, d), y_s.reshape(bs, ss, d)
```

```python
import functools
import math

import jax
import jax.numpy as jnp
from jax import lax
from jax.experimental import pallas as pl
from jax.experimental.pallas import tpu as pltpu

D_MODEL = 2048
N_HEADS = 8
HEAD_DIM = 64
V_DIM = 2 * HEAD_DIM
QK_WIDTH = N_HEADS * 2 * HEAD_DIM
V_WIDTH = N_HEADS * V_DIM
ROPE_THETA = 10000.0
GMLP_GROUPS = 8
GMLP_GROUP_DIM = 128
GMLP_WIDTH = GMLP_GROUPS * GMLP_GROUP_DIM
GMLP_CHUNK = 128
IN_COLS = 3 * 1024 + 2 * 1024 + 2 * D_MODEL
N_GROUPS = 4
EXPERTS_PER_GROUP = 8
N_EXPERTS = N_GROUPS * EXPERTS_PER_GROUP
D_EXPERT = 512
EPS = 1e-6
LAMBDA_INIT = 0.8 - 0.6 * math.exp(-0.3 * 0)

LANES = 128
COL_TILE = 1024
N_COL_TILES = IN_COLS // COL_TILE
VMEM_LIMIT = 56 * 1024 * 1024

TM_IN = 512
TM_MIX = 512
TM_OUT = 256
TQ = 512
TK = 512
TM_EXP = 256
TM_DISP = 512
TM_COMB = 256


def _cparams(sem):
    return pltpu.CompilerParams(dimension_semantics=sem, vmem_limit_bytes=VMEM_LIMIT)


def _rms(x, g):
    return x * lax.rsqrt(jnp.mean(x * x, axis=-1, keepdims=True) + EPS) * g


def _ada_kernel(c_ref, w_ref, b_ref, o_ref):
    c = c_ref[...]
    s = (c / (1.0 + jnp.exp(-c))).astype(jnp.bfloat16)
    o_ref[...] = jnp.dot(s, w_ref[...].astype(jnp.bfloat16),
                         preferred_element_type=jnp.float32) + b_ref[...]


def _ada(c8, w_ada, b_ada):
    n = w_ada.shape[1]
    tn = 1024
    return pl.pallas_call(
        _ada_kernel,
        grid=(n // tn,),
        in_specs=[pl.BlockSpec((8, D_MODEL), lambda j: (0, 0)),
                  pl.BlockSpec((D_MODEL, tn), lambda j: (0, j)),
                  pl.BlockSpec((1, tn), lambda j: (0, j))],
        out_specs=pl.BlockSpec((8, tn), lambda j: (0, j)),
        out_shape=jax.ShapeDtypeStruct((8, n), jnp.float32),
        compiler_params=_cparams(("arbitrary",)),
        name="ada",
    )(c8, w_ada, b_ada)


class _Rows:
    def __init__(self, n_prompt, n_sample_seq, sample_len):
        self.np = n_prompt
        self.ns = n_sample_seq
        self.ls = sample_len
        self.t = n_prompt + n_sample_seq * sample_len

    def seq_of_tile(self, i, tm):
        npt = self.np // tm
        return jnp.where(i < npt, 0, 1 + (i - npt) // (self.ls // tm))

    def pos_tile(self, i, tm):
        npt = self.np // tm
        return jnp.where(i < npt, i, (i - npt) % (self.ls // tm))

    def prompt_tile(self, i, tm):
        return jnp.minimum(i, self.np // tm - 1)

    def sample_tile(self, i, tm):
        return jnp.maximum(i - self.np // tm, 0)


def _inproj_kernel(rows, xp_ref, xs_ref, mod_ref, g1_ref, w_ref, cos_ref, sin_ref, gn_ref,
                   z_ref, h_ref, acc_ref):
    i = pl.program_id(0)
    j = pl.program_id(1)
    tm = h_ref.shape[0]

    def fill_h(x_ref):
        m = mod_ref[0]
        h = _rms(x_ref[...], g1_ref[...]) * (1.0 + m[1:2, :]) + m[0:1, :]
        h_ref[...] = h.astype(jnp.bfloat16)

    @pl.when((j == 0) & (i < rows.np // tm))
    def _():
        fill_h(xp_ref)

    @pl.when((j == 0) & (i >= rows.np // tm))
    def _():
        fill_h(xs_ref)

    acc_ref[...] = jnp.dot(h_ref[...], w_ref[...], preferred_element_type=jnp.float32)

    def rope(scale):
        c = cos_ref[...]
        s = sin_ref[...]
        lane = lax.broadcasted_iota(jnp.int32, (tm, LANES), 1)
        first_half = (lane % HEAD_DIM) < (HEAD_DIM // 2)
        for b in range(COL_TILE // LANES):
            zb = acc_ref[:, b * LANES:(b + 1) * LANES]
            up = pltpu.roll(zb, LANES - HEAD_DIM // 2, 1)
            dn = pltpu.roll(zb, HEAD_DIM // 2, 1)
            r = (zb * c + jnp.where(first_half, up, dn) * s) * scale
            z_ref[:, b * LANES:(b + 1) * LANES] = r.astype(jnp.bfloat16)

    @pl.when(j == 0)
    def _():
        rope(HEAD_DIM ** -0.5 * math.log2(math.e))

    @pl.when(j == 1)
    def _():
        rope(1.0)

    @pl.when(j == 2)
    def _():
        z_ref[...] = acc_ref[...].astype(jnp.bfloat16)

    @pl.when(j == 3)
    def _():
        z_ref[...] = jax.nn.gelu(acc_ref[...], approximate=True).astype(jnp.bfloat16)

    @pl.when(j == 4)
    def _():
        z_ref[...] = _rms(jax.nn.gelu(acc_ref[...], approximate=True), gn_ref[...]).astype(jnp.bfloat16)

    @pl.when(j >= 5)
    def _():
        z_ref[...] = (1.0 / (1.0 + jnp.exp(-acc_ref[...]))).astype(jnp.bfloat16)


def _inproj(rows, xp, xs, mod3, g1, w_in_bf, cos_t, sin_t, gn):
    tm = TM_IN
    return pl.pallas_call(
        functools.partial(_inproj_kernel, rows),
        grid=(rows.t // tm, N_COL_TILES),
        in_specs=[
            pl.BlockSpec((tm, D_MODEL), lambda i, j: (rows.prompt_tile(i, tm), 0)),
            pl.BlockSpec((tm, D_MODEL), lambda i, j: (rows.sample_tile(i, tm), 0)),
            pl.BlockSpec((1, 6, D_MODEL), lambda i, j: (rows.seq_of_tile(i, tm), 0, 0)),
            pl.BlockSpec((1, D_MODEL), lambda i, j: (0, 0)),
            pl.BlockSpec((D_MODEL, COL_TILE), lambda i, j: (0, j)),
            pl.BlockSpec((tm, LANES), lambda i, j: (rows.pos_tile(i, tm), 0)),
            pl.BlockSpec((tm, LANES), lambda i, j: (rows.pos_tile(i, tm), 0)),
            pl.BlockSpec((1, COL_TILE), lambda i, j: (0, 0)),
        ],
        out_specs=pl.BlockSpec((tm, COL_TILE), lambda i, j: (i, j)),
        out_shape=jax.ShapeDtypeStruct((rows.t, IN_COLS), jnp.bfloat16),
        scratch_shapes=[pltpu.VMEM((tm, D_MODEL), jnp.bfloat16),
                        pltpu.VMEM((tm, COL_TILE), jnp.float32)],
        compiler_params=_cparams(("arbitrary", "arbitrary")),
        name="inproj",
    )(xp, xs, mod3, g1, w_in_bf, cos_t, sin_t, gn)


def _attn_kernel(n_k, q_ref, k_ref, v_ref, lam_ref, sg_ref, o_ref, qq_ref, acc_ref, m_ref, l_ref,
                 sa_ref, sb_ref):
    tq = q_ref.shape[0]
    q = q_ref[...]
    lane = lax.broadcasted_iota(jnp.int32, q.shape, 1)
    zero = jnp.zeros_like(q)
    qq_ref[0:tq, :] = jnp.where(lane < HEAD_DIM, q, zero)
    qq_ref[tq:2 * tq, :] = jnp.where(lane >= HEAD_DIM, q, zero)
    m_ref[...] = jnp.full(m_ref.shape, -jnp.inf, jnp.float32)
    l_ref[...] = jnp.zeros(l_ref.shape, jnp.float32)
    acc_ref[...] = jnp.zeros(acc_ref.shape, jnp.float32)

    def scores(ik, s_ref):
        off = pl.multiple_of(ik * TK, TK)
        s_ref[...] = lax.dot_general(k_ref[pl.ds(off, TK), :], qq_ref[...], (((1,), (1,)), ((), ())),
                                     preferred_element_type=jnp.float32)

    def accumulate(ik, s_ref):
        off = pl.multiple_of(ik * TK, TK)
        s = s_ref[...]
        m_old = m_ref[...]
        m_new = jnp.maximum(m_old, jnp.max(s, axis=0, keepdims=True))
        p = jnp.exp2(s - m_new)
        alpha = jnp.exp2(m_old - m_new)
        l_ref[...] = alpha * l_ref[...] + jnp.sum(p, axis=0, keepdims=True)
        m_ref[...] = m_new
        pv = lax.dot_general(v_ref[pl.ds(off, TK), :], p.astype(jnp.bfloat16), (((0,), (0,)), ((), ())),
                             preferred_element_type=jnp.float32)
        acc_ref[...] = alpha * acc_ref[...] + pv

    def chunk_pair(ik, last):
        scores(ik + 1, sb_ref)
        accumulate(ik, sa_ref)
        if not last:
            scores(ik + 2, sa_ref)
        accumulate(ik + 1, sb_ref)

    scores(0, sa_ref)

    def body(i, carry):
        chunk_pair(2 * i, False)
        return carry

    lax.fori_loop(0, n_k // 2 - 1, body, 0)
    chunk_pair(n_k - 2, True)

    lp = lam_ref[...]
    lam = (jnp.exp(jnp.sum(lp[0:1, :] * lp[1:2, :], axis=1, keepdims=True))
           - jnp.exp(jnp.sum(lp[2:3, :] * lp[3:4, :], axis=1, keepdims=True)) + LAMBDA_INIT)
    acc = acc_ref[...]
    l = l_ref[...]
    o_t = acc[:, 0:tq] / l[:, 0:tq] - lam * (acc[:, tq:2 * tq] / l[:, tq:2 * tq])
    o = o_t.T
    o_ref[...] = (_rms(o, sg_ref[...]) * (1.0 - LAMBDA_INIT)).astype(jnp.bfloat16)


def _attn(z, lam_p, subln_g, n_seq, seq_len, row0):
    tq = min(TQ, seq_len)
    nq = seq_len // tq
    assert seq_len % (2 * TK) == 0 and row0 % seq_len == 0
    qb0 = row0 // tq
    sb0 = row0 // seq_len
    kern = functools.partial(_attn_kernel, seq_len // TK)
    return pl.pallas_call(
        kern,
        grid=(n_seq, N_HEADS, nq),
        in_specs=[
            pl.BlockSpec((tq, LANES), lambda b, h, iq: (qb0 + b * nq + iq, h)),
            pl.BlockSpec((seq_len, LANES), lambda b, h, iq: (sb0 + b, N_HEADS + h)),
            pl.BlockSpec((seq_len, LANES), lambda b, h, iq: (sb0 + b, 2 * N_HEADS + h)),
            pl.BlockSpec((4, HEAD_DIM), lambda b, h, iq: (0, 0)),
            pl.BlockSpec((1, V_DIM), lambda b, h, iq: (0, 0)),
        ],
        out_specs=pl.BlockSpec((tq, LANES), lambda b, h, iq: (b * nq + iq, h)),
        out_shape=jax.ShapeDtypeStruct((n_seq * seq_len, V_WIDTH), jnp.bfloat16),
        scratch_shapes=[pltpu.VMEM((2 * tq, LANES), jnp.bfloat16),
                        pltpu.VMEM((V_DIM, 2 * tq), jnp.float32),
                        pltpu.VMEM((1, 2 * tq), jnp.float32),
                        pltpu.VMEM((1, 2 * tq), jnp.float32),
                        pltpu.VMEM((TK, 2 * tq), jnp.float32),
                        pltpu.VMEM((TK, 2 * tq), jnp.float32)],
        compiler_params=_cparams(("arbitrary", "arbitrary", "arbitrary")),
        name="attn",
    )(z, z, z, lam_p, subln_g)


def _mix_kernel(ap_ref, as_ref, gu_ref, gv_ref, sa0_ref, sa1_ref, sb0_ref, sb1_ref,
                ws_ref, bs_ref, wba_ref, wbg_ref, o_ref, m_ref, *, n_prompt_tiles):
    i = pl.program_id(0)
    tm = gu_ref.shape[0]
    bias = bs_ref[...]
    for c in range(tm // GMLP_CHUNK):
        r0 = c * GMLP_CHUNK
        for g in range(GMLP_GROUPS):
            c0 = g * GMLP_GROUP_DIM
            mixed = jnp.dot(ws_ref[g], gv_ref[r0:r0 + GMLP_CHUNK, c0:c0 + GMLP_GROUP_DIM],
                            preferred_element_type=jnp.float32) + bias[:, c0:c0 + GMLP_GROUP_DIM]
            gu = gu_ref[r0:r0 + GMLP_CHUNK, c0:c0 + GMLP_GROUP_DIM].astype(jnp.float32)
            m_ref[r0:r0 + GMLP_CHUNK, c0:c0 + GMLP_GROUP_DIM] = (gu * mixed).astype(jnp.bfloat16)

    def finish(a_ref):
        a = a_ref[...]
        m = m_ref[...]
        for nb, (sa_ref, sb_ref) in enumerate(((sa0_ref, sb0_ref), (sa1_ref, sb1_ref))):
            cs = slice(nb * COL_TILE, (nb + 1) * COL_TILE)
            pa = jnp.dot(a, wba_ref[:, cs], preferred_element_type=jnp.float32)
            pb = jnp.dot(m, wbg_ref[:, cs], preferred_element_type=jnp.float32)
            merged = sa_ref[...].astype(jnp.float32) * pa + sb_ref[...].astype(jnp.float32) * pb
            o_ref[:, cs] = merged.astype(jnp.bfloat16)

    @pl.when(i < n_prompt_tiles)
    def _():
        finish(ap_ref)

    @pl.when(i >= n_prompt_tiles)
    def _():
        finish(as_ref)


def _mix(rows, a_p, a_s, z, ws_bf, bias_tile, wba_bf, wbg_bf):
    tm = TM_MIX
    zcol = lambda cb: pl.BlockSpec((tm, COL_TILE), lambda i: (i, cb))
    return pl.pallas_call(
        functools.partial(_mix_kernel, n_prompt_tiles=rows.np // tm),
        grid=(rows.t // tm,),
        in_specs=[
            pl.BlockSpec((tm, V_WIDTH), lambda i: (rows.prompt_tile(i, tm), 0)),
            pl.BlockSpec((tm, V_WIDTH), lambda i: (rows.sample_tile(i, tm), 0)),
            zcol(3), zcol(4), zcol(5), zcol(6), zcol(7), zcol(8),
            pl.BlockSpec((GMLP_GROUPS, GMLP_CHUNK, GMLP_CHUNK), lambda i: (0, 0, 0)),
            pl.BlockSpec((GMLP_CHUNK, GMLP_WIDTH), lambda i: (0, 0)),
            pl.BlockSpec((V_WIDTH, D_MODEL), lambda i: (0, 0)),
            pl.BlockSpec((GMLP_WIDTH, D_MODEL), lambda i: (0, 0)),
        ],
        out_specs=pl.BlockSpec((tm, D_MODEL), lambda i: (i, 0)),
        out_shape=jax.ShapeDtypeStruct((rows.t, D_MODEL), jnp.bfloat16),
        scratch_shapes=[pltpu.VMEM((tm, GMLP_WIDTH), jnp.bfloat16)],
        compiler_params=_cparams(("arbitrary",)),
        name="mix",
    )(a_p, a_s, z, z, z, z, z, z, ws_bf, bias_tile, wba_bf, wbg_bf)


def _outproj_kernel(rows, mg_ref, xp_ref, xs_ref, mod_ref, g2_ref, wo_ref, wrh_ref, wrl_ref, tri_ref,
                    x1_ref, h2_ref, ri_ref, rf_ref, cnt_ref, base_ref):
    i = pl.program_id(0)
    tm = mg_ref.shape[0]

    @pl.when(i == 0)
    def _():
        base_ref[...] = jnp.zeros(base_ref.shape, jnp.float32)

    m = mod_ref[0]
    y = jnp.dot(mg_ref[...], wo_ref[...], preferred_element_type=jnp.float32)

    def residual(x_ref):
        x1 = x_ref[...] + m[2:3, :] * y
        x1_ref[...] = x1
        h2_ref[...] = _rms(x1, g2_ref[...]) * (1.0 + m[4:5, :]) + m[3:4, :]

    @pl.when(i < rows.np // tm)
    def _():
        residual(xp_ref)

    @pl.when(i >= rows.np // tm)
    def _():
        residual(xs_ref)

    h2 = h2_ref[...]
    h_hi = h2.astype(jnp.bfloat16)
    h_lo = (h2 - h_hi.astype(jnp.float32)).astype(jnp.bfloat16)
    nt = (((1,), (1,)), ((), ()))
    lg = (lax.dot_general(wrh_ref[...], h_hi, nt, preferred_element_type=jnp.float32)
          + lax.dot_general(wrh_ref[...], h_lo, nt, preferred_element_type=jnp.float32)
          + lax.dot_general(wrl_ref[...], h_hi, nt, preferred_element_type=jnp.float32))

    row8 = lax.broadcasted_iota(jnp.int32, (8, tm), 0)
    neg = jnp.float32(-jnp.inf)
    gl = jnp.where(row8 < N_GROUPS, lg[N_EXPERTS:N_EXPERTS + 8, :], neg)
    gmax = jnp.max(gl, axis=0, keepdims=True)
    g_top = 1.0 / jnp.sum(jnp.exp(gl - gmax), axis=0, keepdims=True)
    g_idx = jnp.min(jnp.where(gl == gmax, row8, 8), axis=0, keepdims=True)
    e_sel = jnp.zeros((8, tm), jnp.float32)
    for g in range(N_GROUPS):
        e_sel = jnp.where(g_idx == g, lg[g * EXPERTS_PER_GROUP:(g + 1) * EXPERTS_PER_GROUP, :], e_sel)
    e1 = jnp.max(e_sel, axis=0, keepdims=True)
    i1 = jnp.min(jnp.where(e_sel == e1, row8, 8), axis=0, keepdims=True)
    e_rest = jnp.where(row8 == i1, neg, e_sel)
    e2 = jnp.max(e_rest, axis=0, keepdims=True)
    i2 = jnp.min(jnp.where(e_rest == e2, row8, 8), axis=0, keepdims=True)
    t = jnp.exp(e2 - e1)
    w1 = g_top / (1.0 + t)
    w2 = g_top * t / (1.0 + t)
    id1 = g_idx * EXPERTS_PER_GROUP + i1
    id2 = g_idx * EXPERTS_PER_GROUP + i2

    row32 = lax.broadcasted_iota(jnp.int32, (N_EXPERTS, tm), 0)
    oh1 = (row32 == id1).astype(jnp.float32)
    oh2 = (row32 == id2).astype(jnp.float32)
    both = oh1 + oh2
    before = jnp.dot(both.astype(jnp.bfloat16), tri_ref[...], preferred_element_type=jnp.float32)
    base = base_ref[...]
    tot = before + jnp.concatenate([base] * (tm // LANES), axis=1)
    rank1 = jnp.sum(oh1 * tot, axis=0, keepdims=True)
    rank2 = jnp.sum(oh2 * tot, axis=0, keepdims=True)
    new_base = base + jnp.sum(both, axis=1, keepdims=True)
    base_ref[...] = new_base
    cnt_ref[...] = new_base.astype(jnp.int32)

    zi = jnp.zeros((4, tm), jnp.int32)
    ri_ref[...] = jnp.concatenate([id1, id2, rank1.astype(jnp.int32), rank2.astype(jnp.int32), zi], axis=0)
    rf_ref[...] = jnp.concatenate([w1, w2, jnp.zeros((6, tm), jnp.float32)], axis=0)


def _outproj(rows, merged, xp, xs, mod3, g2, wo_bf, wr_hi, wr_lo, tri):
    tm = TM_OUT
    t = rows.t
    return pl.pallas_call(
        functools.partial(_outproj_kernel, rows),
        grid=(t // tm,),
        in_specs=[
            pl.BlockSpec((tm, D_MODEL), lambda i: (i, 0)),
            pl.BlockSpec((tm, D_MODEL), lambda i: (rows.prompt_tile(i, tm), 0)),
            pl.BlockSpec((tm, D_MODEL), lambda i: (rows.sample_tile(i, tm), 0)),
            pl.BlockSpec((1, 6, D_MODEL), lambda i: (rows.seq_of_tile(i, tm), 0, 0)),
            pl.BlockSpec((1, D_MODEL), lambda i: (0, 0)),
            pl.BlockSpec((D_MODEL, D_MODEL), lambda i: (0, 0)),
            pl.BlockSpec((N_EXPERTS + 8, D_MODEL), lambda i: (0, 0)),
            pl.BlockSpec((N_EXPERTS + 8, D_MODEL), lambda i: (0, 0)),
            pl.BlockSpec((tm, tm), lambda i: (0, 0)),
        ],
        out_specs=[
            pl.BlockSpec((tm, D_MODEL), lambda i: (i, 0)),
            pl.BlockSpec((tm, D_MODEL), lambda i: (i, 0)),
            pl.BlockSpec((8, tm), lambda i: (0, i)),
            pl.BlockSpec((8, tm), lambda i: (0, i)),
            pl.BlockSpec((N_EXPERTS, LANES), lambda i: (0, 0)),
        ],
        out_shape=[
            jax.ShapeDtypeStruct((t, D_MODEL), jnp.float32),
            jax.ShapeDtypeStruct((t, D_MODEL), jnp.float32),
            jax.ShapeDtypeStruct((8, t), jnp.int32),
            jax.ShapeDtypeStruct((8, t), jnp.float32),
            jax.ShapeDtypeStruct((N_EXPERTS, LANES), jnp.int32),
        ],
        scratch_shapes=[pltpu.VMEM((N_EXPERTS, LANES), jnp.float32)],
        compiler_params=_cparams(("arbitrary",)),
        name="outproj",
    )(merged, xp, xs, mod3, g2, wo_bf, wr_hi, wr_lo, tri)


def _dispatch_kernel(pos_ref, h2_ref, init_ref, xs_ref, sem):
    del init_ref
    i = pl.program_id(0)
    t_total = pl.num_programs(0) * TM_DISP
    t0 = i * TM_DISP

    def row_copy(r, dst):
        return pltpu.make_async_copy(h2_ref.at[pl.ds(r, 1), :], xs_ref.at[pl.ds(dst, 1), :], sem)

    def issue(r, carry):
        tok = t0 + r
        row_copy(r, pos_ref[tok]).start()
        row_copy(r, pos_ref[t_total + tok]).start()
        return carry

    lax.fori_loop(0, TM_DISP, issue, 0, unroll=8)

    def drain(r, carry):
        row_copy(0, 0).wait()
        row_copy(0, 0).wait()
        return carry

    lax.fori_loop(0, TM_DISP, drain, 0, unroll=8)


def _dispatch(pos_flat, h2, n_rows):
    t = h2.shape[0]
    init = jnp.zeros((n_rows, D_MODEL), jnp.float32)
    return pl.pallas_call(
        _dispatch_kernel,
        grid_spec=pltpu.PrefetchScalarGridSpec(
            num_scalar_prefetch=1,
            grid=(t // TM_DISP,),
            in_specs=[pl.BlockSpec((TM_DISP, D_MODEL), lambda i, pos: (i, 0)),
                      pl.BlockSpec(memory_space=pl.ANY)],
            out_specs=pl.BlockSpec(memory_space=pl.ANY),
            scratch_shapes=[pltpu.SemaphoreType.DMA],
        ),
        out_shape=jax.ShapeDtypeStruct((n_rows, D_MODEL), jnp.float32),
        input_output_aliases={2: 0},
        compiler_params=_cparams(("arbitrary",)),
        name="dispatch",
    )(pos_flat, h2, init)


def _experts_kernel(meta_ref, x_ref, wg_ref, wu_ref, wd_ref, y_ref):
    j = pl.program_id(0)

    @pl.when(j < meta_ref[0])
    def _():
        x = x_ref[...].astype(jnp.bfloat16)
        g = jnp.dot(x, wg_ref[0], preferred_element_type=jnp.float32)
        u = jnp.dot(x, wu_ref[0], preferred_element_type=jnp.float32)
        hm = (g / (1.0 + jnp.exp(-g)) * u).astype(jnp.bfloat16)
        y_ref[...] = jnp.dot(hm, wd_ref[0], preferred_element_type=jnp.float32)

    @pl.when(j >= meta_ref[0])
    def _():
        y_ref[...] = jnp.zeros(y_ref.shape, jnp.float32)


def _experts(meta, xs, wg_bf, wu_bf, wd_bf):
    n_rows = xs.shape[0]
    n_tiles = n_rows // TM_EXP
    row = lambda j, meta: (jnp.minimum(j, meta[0] - 1), 0)
    wsel = lambda j, meta: (meta[1 + j], 0, 0)
    return pl.pallas_call(
        _experts_kernel,
        grid_spec=pltpu.PrefetchScalarGridSpec(
            num_scalar_prefetch=1,
            grid=(n_tiles,),
            in_specs=[pl.BlockSpec((TM_EXP, D_MODEL), row),
                      pl.BlockSpec((1, D_MODEL, D_EXPERT), wsel),
                      pl.BlockSpec((1, D_MODEL, D_EXPERT), wsel),
                      pl.BlockSpec((1, D_EXPERT, D_MODEL), wsel)],
            out_specs=pl.BlockSpec((TM_EXP, D_MODEL), lambda j, meta: (j, 0)),
        ),
        out_shape=jax.ShapeDtypeStruct((n_rows, D_MODEL), jnp.float32),
        compiler_params=_cparams(("arbitrary",)),
        name="experts",
    )(meta, xs, wg_bf, wu_bf, wd_bf)


def _combine_kernel(rows, pos_ref, ys_ref, x1_ref, rf_ref, mod_ref, gf_ref, op_ref, os_ref, buf_ref, sem):
    i = pl.program_id(0)
    tm = x1_ref.shape[0]
    t0 = i * tm

    def row_copy(src, slot, r):
        return pltpu.make_async_copy(ys_ref.at[pl.ds(src, 1), :], buf_ref.at[slot, pl.ds(r, 1), :], sem)

    def issue(r, carry):
        row_copy(pos_ref[t0 + r], 0, r).start()
        row_copy(pos_ref[rows.t + t0 + r], 1, r).start()
        return carry

    lax.fori_loop(0, tm, issue, 0, unroll=8)

    def drain(r, carry):
        row_copy(0, 0, 0).wait()
        row_copy(0, 1, 0).wait()
        return carry

    lax.fori_loop(0, tm, drain, 0, unroll=8)

    w = rf_ref[...].T
    moe = w[:, 0:1] * buf_ref[0] + w[:, 1:2] * buf_ref[1]
    x2 = x1_ref[...] + mod_ref[0][5:6, :] * moe
    out = _rms(x2, gf_ref[...])

    @pl.when(i < rows.np // tm)
    def _():
        op_ref[...] = out

    @pl.when(i >= rows.np // tm)
    def _():
        os_ref[...] = out


def _combine(rows, pos_flat, ys, x1, rf, mod3, gf):
    tm = TM_COMB
    return pl.pallas_call(
        functools.partial(_combine_kernel, rows),
        grid_spec=pltpu.PrefetchScalarGridSpec(
            num_scalar_prefetch=1,
            grid=(rows.t // tm,),
            in_specs=[
                pl.BlockSpec(memory_space=pl.ANY),
                pl.BlockSpec((tm, D_MODEL), lambda i, pos: (i, 0)),
                pl.BlockSpec((8, tm), lambda i, pos: (0, i)),
                pl.BlockSpec((1, 6, D_MODEL), lambda i, pos: (rows.seq_of_tile(i, tm), 0, 0)),
                pl.BlockSpec((1, D_MODEL), lambda i, pos: (0, 0)),
            ],
            out_specs=[
                pl.BlockSpec((tm, D_MODEL), lambda i, pos: (rows.prompt_tile(i, tm), 0)),
                pl.BlockSpec((tm, D_MODEL), lambda i, pos: (rows.sample_tile(i, tm), 0)),
            ],
            scratch_shapes=[pltpu.VMEM((2, tm, D_MODEL), jnp.float32), pltpu.SemaphoreType.DMA],
        ),
        out_shape=[jax.ShapeDtypeStruct((rows.np, D_MODEL), jnp.float32),
                   jax.ShapeDtypeStruct((rows.t - rows.np, D_MODEL), jnp.float32)],
        compiler_params=_cparams(("arbitrary",)),
        name="combine",
    )(pos_flat, ys, x1, rf, mod3, gf)


def _rope_tables(seq_len):
    pos = jnp.arange(seq_len, dtype=jnp.float32)
    inv = 1.0 / (ROPE_THETA ** (jnp.arange(0, HEAD_DIM, 2, dtype=jnp.float32) / HEAD_DIM))
    ang = pos[:, None] * inv[None, :]
    cos, sin = jnp.cos(ang), jnp.sin(ang)
    cos_t = jnp.tile(cos, (1, LANES // (HEAD_DIM // 2)))
    sin_t = jnp.tile(jnp.concatenate([-sin, sin], axis=1), (1, LANES // HEAD_DIM))
    return cos_t, sin_t


def _routing_tables(ri, cnt, n_tiles_max):
    counts = cnt[:, 0]
    padded = (counts + TM_EXP - 1) // TM_EXP * TM_EXP
    seg_end = jnp.cumsum(padded)
    seg_start = seg_end - padded
    pos1 = seg_start[ri[0]] + ri[2]
    pos2 = seg_start[ri[1]] + ri[3]
    n_active = seg_end[-1] // TM_EXP
    tile_row = jnp.minimum(jnp.arange(n_tiles_max, dtype=jnp.int32), n_active - 1) * TM_EXP
    tile_expert = jnp.sum((tile_row[:, None] >= seg_end[None, :]).astype(jnp.int32), axis=1)
    meta = jnp.concatenate([n_active[None].astype(jnp.int32), tile_expert.astype(jnp.int32)])
    return jnp.concatenate([pos1, pos2]).astype(jnp.int32), meta


def kernel(x_prompt, x_sample, c_prompt, c_sample, w_ada, b_ada, norm1_g, w_in, lambda_q1, lambda_k1,
           lambda_q2, lambda_k2, subln_g, gmlp_norm_g, w_spatial, b_spatial, w_branch_att,
           w_branch_gmlp, w_out, norm2_g, w_router_group, w_router_expert, w_gate, w_up, w_down,
           final_norm_g):
    bf = jnp.bfloat16
    bp, sp, d = x_prompt.shape
    bs, ss, _ = x_sample.shape
    assert bp == 1 and d == D_MODEL
    rows = _Rows(bp * sp, bs, ss)
    xp = x_prompt.reshape(bp * sp, d)
    xs = x_sample.reshape(bs * ss, d)

    c8 = jnp.concatenate([c_prompt, c_sample, jnp.zeros((8 - bp - bs, d), jnp.float32)], axis=0)
    mod3 = _ada(c8, w_ada[0], b_ada).reshape(8, 6, d)

    cos_t, sin_t = _rope_tables(max(sp, ss))
    z = _inproj(rows, xp, xs, mod3, norm1_g, w_in[0].astype(bf), cos_t, sin_t, gmlp_norm_g)

    lam_p = jnp.concatenate([lambda_q1, lambda_k1, lambda_q2, lambda_k2], axis=0)
    a_p = _attn(z, lam_p, subln_g, bp, sp, 0)
    a_s = _attn(z, lam_p, subln_g, bs, ss, bp * sp)

    bias_tile = jnp.repeat(b_spatial[0].T, GMLP_GROUP_DIM, axis=1)
    merged = _mix(rows, a_p, a_s, z, w_spatial[0].astype(bf), bias_tile,
                  w_branch_att[0].astype(bf), w_branch_gmlp[0].astype(bf))

    wr_t = jnp.concatenate([w_router_expert[0].T, w_router_group[0].T,
                            jnp.zeros((8 - N_GROUPS, d), jnp.float32)], axis=0)
    wr_hi = wr_t.astype(bf)
    wr_lo = (wr_t - wr_hi.astype(jnp.float32)).astype(bf)
    tri = jnp.triu(jnp.ones((TM_OUT, TM_OUT), bf), k=1)
    x1, h2, ri, rf, cnt = _outproj(rows, merged, xp, xs, mod3, norm2_g, w_out[0].astype(bf),
                                   wr_hi, wr_lo, tri)

    n_rows = (2 * rows.t // TM_EXP + N_EXPERTS) * TM_EXP
    pos_flat, meta = _routing_tables(ri, cnt, n_rows // TM_EXP)
    xs_sorted = _dispatch(pos_flat, h2, n_rows)
    ys = _experts(meta, xs_sorted, w_gate[0].astype(bf), w_up[0].astype(bf), w_down[0].astype(bf))
    y_p, y_s = _combine(rows, pos_flat, ys, x1, rf, mod3, final_norm_g.reshape(1, d))
    return y_p.reshape(bp, sp, d), y_s.reshape(bs, ss, d)
```

```python
import functools
import math

import jax
import jax.numpy as jnp
from jax import lax
from jax.experimental import pallas as pl
from jax.experimental.pallas import tpu as pltpu

D_MODEL = 2048
N_HEADS = 8
HEAD_DIM = 64
V_DIM = 2 * HEAD_DIM
QK_WIDTH = N_HEADS * 2 * HEAD_DIM
V_WIDTH = N_HEADS * V_DIM
ROPE_THETA = 10000.0
GMLP_GROUPS = 8
GMLP_GROUP_DIM = 128
GMLP_WIDTH = GMLP_GROUPS * GMLP_GROUP_DIM
GMLP_CHUNK = 128
IN_COLS = 3 * 1024 + 2 * 1024 + 2 * D_MODEL
N_GROUPS = 4
EXPERTS_PER_GROUP = 8
N_EXPERTS = N_GROUPS * EXPERTS_PER_GROUP
D_EXPERT = 512
EPS = 1e-6
LAMBDA_INIT = 0.8 - 0.6 * math.exp(-0.3 * 0)

LANES = 128
COL_TILE = 1024
N_COL_TILES = IN_COLS // COL_TILE
VMEM_LIMIT = 56 * 1024 * 1024

TM_IN = 512
TM_MIX = 512
TM_OUT = 256
TQ = 512
TK = 1024
TM_EXP = 256
TM_DISP = 512
TM_COMB = 256


def _cparams(sem, flags=None):
    return pltpu.CompilerParams(dimension_semantics=sem, vmem_limit_bytes=VMEM_LIMIT, flags=flags)


def _rms(x, g):
    return x * lax.rsqrt(jnp.mean(x * x, axis=-1, keepdims=True) + EPS) * g


def _pack_rows(x):
    n = x.shape[1] // 2
    hi = lax.bitcast_convert_type(x[:, :n].astype(jnp.bfloat16).astype(jnp.float32), jnp.uint32)
    lo = lax.bitcast_convert_type(x[:, n:].astype(jnp.bfloat16).astype(jnp.float32), jnp.uint32)
    return hi | (lo >> 16)


def _unpack_rows(w):
    a = lax.bitcast_convert_type(w & jnp.uint32(0xFFFF0000), jnp.float32)
    b = lax.bitcast_convert_type(w << 16, jnp.float32)
    return a, b


def _ada_kernel(c_ref, w_ref, b_ref, o_ref):
    c = c_ref[...]
    s = (c / (1.0 + jnp.exp(-c))).astype(jnp.bfloat16)
    o_ref[...] = jnp.dot(s, w_ref[...].astype(jnp.bfloat16),
                         preferred_element_type=jnp.float32) + b_ref[...]


def _ada(c8, w_ada, b_ada):
    n = w_ada.shape[1]
    tn = 1024
    return pl.pallas_call(
        _ada_kernel,
        grid=(n // tn,),
        in_specs=[pl.BlockSpec((8, D_MODEL), lambda j: (0, 0)),
                  pl.BlockSpec((D_MODEL, tn), lambda j: (0, j)),
                  pl.BlockSpec((1, tn), lambda j: (0, j))],
        out_specs=pl.BlockSpec((8, tn), lambda j: (0, j)),
        out_shape=jax.ShapeDtypeStruct((8, n), jnp.float32),
        compiler_params=_cparams(("arbitrary",)),
        name="ada",
    )(c8, w_ada, b_ada)


class _Rows:
    def __init__(self, n_prompt, n_sample_seq, sample_len):
        self.np = n_prompt
        self.ns = n_sample_seq
        self.ls = sample_len
        self.t = n_prompt + n_sample_seq * sample_len

    def seq_of_tile(self, i, tm):
        npt = self.np // tm
        return jnp.where(i < npt, 0, 1 + (i - npt) // (self.ls // tm))

    def pos_tile(self, i, tm):
        npt = self.np // tm
        return jnp.where(i < npt, i, (i - npt) % (self.ls // tm))

    def prompt_tile(self, i, tm):
        return jnp.minimum(i, self.np // tm - 1)

    def sample_tile(self, i, tm):
        return jnp.maximum(i - self.np // tm, 0)


def _inproj_kernel(rows, xp_ref, xs_ref, mod_ref, g1_ref, w_ref, cos_ref, sin_ref, gn_ref,
                   z_ref, h_ref):
    i = pl.program_id(0)
    j = pl.program_id(1)
    tm = h_ref.shape[0]

    @pl.when(j == 0)
    def _():
        m = mod_ref[0]
        x = jnp.where(i < rows.np // tm, xp_ref[...], xs_ref[...])
        h = _rms(x, g1_ref[...]) * (1.0 + m[1:2, :]) + m[0:1, :]
        h_ref[...] = h.astype(jnp.bfloat16)

    def zdot():
        return jnp.dot(h_ref[...], w_ref[...], preferred_element_type=jnp.float32)

    def rope(scale):
        z = zdot()
        c = cos_ref[...]
        s = sin_ref[...]
        lane = lax.broadcasted_iota(jnp.int32, (tm, LANES), 1)
        first_half = (lane % HEAD_DIM) < (HEAD_DIM // 2)
        for b in range(COL_TILE // LANES):
            zb = z[:, b * LANES:(b + 1) * LANES]
            up = pltpu.roll(zb, LANES - HEAD_DIM // 2, 1)
            dn = pltpu.roll(zb, HEAD_DIM // 2, 1)
            r = (zb * c + jnp.where(first_half, up, dn) * s) * scale
            z_ref[:, b * LANES:(b + 1) * LANES] = r.astype(jnp.bfloat16)

    @pl.when(j == 0)
    def _():
        rope(HEAD_DIM ** -0.5 * math.log2(math.e))

    @pl.when(j == 1)
    def _():
        rope(1.0)

    @pl.when(j == 2)
    def _():
        z_ref[...] = zdot().astype(jnp.bfloat16)

    @pl.when(j == 3)
    def _():
        z_ref[...] = jax.nn.gelu(zdot(), approximate=True).astype(jnp.bfloat16)

    @pl.when(j == 4)
    def _():
        z_ref[...] = _rms(jax.nn.gelu(zdot(), approximate=True), gn_ref[...]).astype(jnp.bfloat16)

    @pl.when(j >= 5)
    def _():
        z_ref[...] = (1.0 / (1.0 + jnp.exp(-zdot()))).astype(jnp.bfloat16)


def _inproj(rows, xp, xs, mod3, g1, w_in_bf, cos_t, sin_t, gn):
    tm = TM_IN
    return pl.pallas_call(
        functools.partial(_inproj_kernel, rows),
        grid=(rows.t // tm, N_COL_TILES),
        in_specs=[
            pl.BlockSpec((tm, D_MODEL), lambda i, j: (rows.prompt_tile(i, tm), 0)),
            pl.BlockSpec((tm, D_MODEL), lambda i, j: (rows.sample_tile(i, tm), 0)),
            pl.BlockSpec((1, 6, D_MODEL), lambda i, j: (rows.seq_of_tile(i, tm), 0, 0)),
            pl.BlockSpec((1, D_MODEL), lambda i, j: (0, 0)),
            pl.BlockSpec((D_MODEL, COL_TILE), lambda i, j: (0, j)),
            pl.BlockSpec((tm, LANES), lambda i, j: (rows.pos_tile(i, tm), 0)),
            pl.BlockSpec((tm, LANES), lambda i, j: (rows.pos_tile(i, tm), 0)),
            pl.BlockSpec((1, COL_TILE), lambda i, j: (0, 0)),
        ],
        out_specs=pl.BlockSpec((tm, COL_TILE), lambda i, j: (i, j)),
        out_shape=jax.ShapeDtypeStruct((rows.t, IN_COLS), jnp.bfloat16),
        scratch_shapes=[pltpu.VMEM((tm, D_MODEL), jnp.bfloat16)],
        compiler_params=_cparams(("arbitrary", "arbitrary")),
        name="inproj",
    )(xp, xs, mod3, g1, w_in_bf, cos_t, sin_t, gn)


ONES_ROWS = 16


def _attn_kernel(n_k, q_ref, k_ref, v_ref, lam_ref, sg_ref, o_ref, qq_ref, vt_ref, acc_ref, m_ref,
                 sa_ref, sb_ref):
    tq = q_ref.shape[0]

    @pl.when(pl.program_id(2) == 0)
    def _():
        vt_ref[V_DIM:, :] = jnp.ones((ONES_ROWS, vt_ref.shape[1]), jnp.bfloat16)

        def transpose_chunk(c, carry):
            off = pl.multiple_of(c * TK, TK)
            vc = v_ref[pl.ds(off, TK), :].astype(jnp.float32)
            vt_ref[0:V_DIM, pl.ds(off, TK)] = vc.T.astype(jnp.bfloat16)
            return carry

        lax.fori_loop(0, n_k, transpose_chunk, 0)

    q = q_ref[...]
    lane = lax.broadcasted_iota(jnp.int32, q.shape, 1)
    zero = jnp.zeros_like(q)
    qq_ref[0:tq, :] = jnp.where(lane < HEAD_DIM, q, zero)
    qq_ref[tq:2 * tq, :] = jnp.where(lane >= HEAD_DIM, q, zero)
    m_ref[...] = jnp.full(m_ref.shape, -jnp.inf, jnp.float32)
    acc_ref[...] = jnp.zeros(acc_ref.shape, jnp.float32)

    def scores(ik, s_ref):
        off = pl.multiple_of(ik * TK, TK)
        s_ref[...] = lax.dot_general(k_ref[pl.ds(off, TK), :], qq_ref[...], (((1,), (1,)), ((), ())),
                                     preferred_element_type=jnp.float32)

    def accumulate(ik, s_ref):
        off = pl.multiple_of(ik * TK, TK)
        s = s_ref[...]
        m_old = m_ref[...]
        m_new = jnp.maximum(m_old, jnp.max(s, axis=0, keepdims=True))
        p = jnp.exp2(s - m_new).astype(jnp.bfloat16)
        alpha = jnp.exp2(m_old - m_new)
        m_ref[...] = m_new
        pv = jnp.dot(vt_ref[:, pl.ds(off, TK)], p, preferred_element_type=jnp.float32)
        acc_ref[...] = alpha * acc_ref[...] + pv

    def chunk_pair(ik, last):
        scores(ik + 1, sb_ref)
        accumulate(ik, sa_ref)
        if not last:
            scores(ik + 2, sa_ref)
        accumulate(ik + 1, sb_ref)

    scores(0, sa_ref)

    def body(i, carry):
        chunk_pair(2 * i, False)
        return carry

    lax.fori_loop(0, n_k // 2 - 1, body, 0)
    chunk_pair(n_k - 2, True)

    lp = lam_ref[...]
    lam = (jnp.exp(jnp.sum(lp[0:1, :] * lp[1:2, :], axis=1, keepdims=True))
           - jnp.exp(jnp.sum(lp[2:3, :] * lp[3:4, :], axis=1, keepdims=True)) + LAMBDA_INIT)
    acc = acc_ref[0:V_DIM, :]
    l = acc_ref[V_DIM:V_DIM + 1, :]
    o_t =acc[:, 0:tq] / l[:, 0:tq] - lam * (acc[:, tq:2 * tq] / l[:, tq:2 * tq])
    o = o_t.T
    o_ref[...] = (_rms(o, sg_ref[...]) * (1.0 - LAMBDA_INIT)).astype(jnp.bfloat16)


def _attn(z, lam_p, subln_g, n_seq, seq_len, row0):
    tq = min(TQ, seq_len)
    nq = seq_len // tq
    assert seq_len % (2 * TK) == 0 and row0 % seq_len == 0
    qb0 = row0 // tq
    sb0 = row0 // seq_len
    kern = functools.partial(_attn_kernel, seq_len // TK)
    return pl.pallas_call(
        kern,
        grid=(n_seq, N_HEADS, nq),
        in_specs=[
            pl.BlockSpec((tq, LANES), lambda b, h, iq: (qb0 + b * nq + iq, h)),
            pl.BlockSpec((seq_len, LANES), lambda b, h, iq: (sb0 + b, N_HEADS + h)),
            pl.BlockSpec((seq_len, LANES), lambda b, h, iq: (sb0 + b, 2 * N_HEADS + h)),
            pl.BlockSpec((4, HEAD_DIM), lambda b, h, iq: (0, 0)),
            pl.BlockSpec((1, V_DIM), lambda b, h, iq: (0, 0)),
        ],
        out_specs=pl.BlockSpec((tq, LANES), lambda b, h, iq: (b * nq + iq, h)),
        out_shape=jax.ShapeDtypeStruct((n_seq * seq_len, V_WIDTH), jnp.bfloat16),
        scratch_shapes=[pltpu.VMEM((2 * tq, LANES), jnp.bfloat16),
                        pltpu.VMEM((V_DIM + ONES_ROWS, seq_len), jnp.bfloat16),
                        pltpu.VMEM((V_DIM + ONES_ROWS, 2 * tq), jnp.float32),
                        pltpu.VMEM((1, 2 * tq), jnp.float32),
                        pltpu.VMEM((TK, 2 * tq), jnp.float32),
                        pltpu.VMEM((TK, 2 * tq), jnp.float32)],
        compiler_params=_cparams(("arbitrary", "arbitrary", "arbitrary")),
        name="attn",
    )(z, z, z, lam_p, subln_g)


def _mix_kernel(ap_ref, as_ref, gu_ref, gv_ref, sa0_ref, sa1_ref, sb0_ref, sb1_ref,
                ws_ref, bs_ref, wba_ref, wbg_ref, o_ref, m_ref, *, n_prompt_tiles):
    i = pl.program_id(0)
    tm = gu_ref.shape[0]
    bias = bs_ref[...]
    for c in range(tm // GMLP_CHUNK):
        r0 = c * GMLP_CHUNK
        for g in range(GMLP_GROUPS):
            c0 = g * GMLP_GROUP_DIM
            mixed = jnp.dot(ws_ref[g], gv_ref[r0:r0 + GMLP_CHUNK, c0:c0 + GMLP_GROUP_DIM],
                            preferred_element_type=jnp.float32) + bias[:, c0:c0 + GMLP_GROUP_DIM]
            gu = gu_ref[r0:r0 + GMLP_CHUNK, c0:c0 + GMLP_GROUP_DIM].astype(jnp.float32)
            m_ref[r0:r0 + GMLP_CHUNK, c0:c0 + GMLP_GROUP_DIM] = (gu * mixed).astype(jnp.bfloat16)

    a = jnp.where(i < n_prompt_tiles, ap_ref[...], as_ref[...])
    m = m_ref[...]
    for nb, (sa_ref, sb_ref) in enumerate(((sa0_ref, sb0_ref), (sa1_ref, sb1_ref))):
        cs = slice(nb * COL_TILE, (nb + 1) * COL_TILE)
        pa = jnp.dot(a, wba_ref[:, cs], preferred_element_type=jnp.float32)
        pb = jnp.dot(m, wbg_ref[:, cs], preferred_element_type=jnp.float32)
        merged = sa_ref[...].astype(jnp.float32) * pa + sb_ref[...].astype(jnp.float32) * pb
        o_ref[:, cs] = merged.astype(jnp.bfloat16)


def _mix(rows, a_p, a_s, z, ws_bf, bias_tile, wba_bf, wbg_bf):
    tm = TM_MIX
    zcol = lambda cb: pl.BlockSpec((tm, COL_TILE), lambda i: (i, cb))
    return pl.pallas_call(
        functools.partial(_mix_kernel, n_prompt_tiles=rows.np // tm),
        grid=(rows.t // tm,),
        in_specs=[
            pl.BlockSpec((tm, V_WIDTH), lambda i: (rows.prompt_tile(i, tm), 0)),
            pl.BlockSpec((tm, V_WIDTH), lambda i: (rows.sample_tile(i, tm), 0)),
            zcol(3), zcol(4), zcol(5), zcol(6), zcol(7), zcol(8),
            pl.BlockSpec((GMLP_GROUPS, GMLP_CHUNK, GMLP_CHUNK), lambda i: (0, 0, 0)),
            pl.BlockSpec((GMLP_CHUNK, GMLP_WIDTH), lambda i: (0, 0)),
            pl.BlockSpec((V_WIDTH, D_MODEL), lambda i: (0, 0)),
            pl.BlockSpec((GMLP_WIDTH, D_MODEL), lambda i: (0, 0)),
        ],
        out_specs=pl.BlockSpec((tm, D_MODEL), lambda i: (i, 0)),
        out_shape=jax.ShapeDtypeStruct((rows.t, D_MODEL), jnp.bfloat16),
        scratch_shapes=[pltpu.VMEM((tm, GMLP_WIDTH), jnp.bfloat16)],
        compiler_params=_cparams(("arbitrary",)),
        name="mix",
    )(a_p, a_s, z, z, z, z, z, z, ws_bf, bias_tile, wba_bf, wbg_bf)


def _outproj_kernel(rows, mg_ref, xp_ref, xs_ref, mod_ref, g2_ref, wo_ref, wrh_ref, wrl_ref, tri_ref,
                    x1_ref, h2_ref, ri_ref, rf_ref, cnt_ref, base_ref):
    i = pl.program_id(0)
    tm = mg_ref.shape[0]

    @pl.when(i == 0)
    def _():
        base_ref[...] = jnp.zeros(base_ref.shape, jnp.float32)

    m = mod_ref[0]
    y = jnp.dot(mg_ref[...], wo_ref[...], preferred_element_type=jnp.float32)

    x = jnp.where(i < rows.np // tm, xp_ref[...], xs_ref[...])
    x1 = x + m[2:3, :] * y
    x1_ref[...] = x1
    h2 = _rms(x1, g2_ref[...]) * (1.0 + m[4:5, :]) + m[3:4, :]
    h2_ref[...] = _pack_rows(h2)

    h_hi = h2.astype(jnp.bfloat16)
    h_lo = (h2 - h_hi.astype(jnp.float32)).astype(jnp.bfloat16)
    nt = (((1,), (1,)), ((), ()))
    lg = (lax.dot_general(wrh_ref[...], h_hi, nt, preferred_element_type=jnp.float32)
          + lax.dot_general(wrh_ref[...], h_lo, nt, preferred_element_type=jnp.float32)
          + lax.dot_general(wrl_ref[...], h_hi, nt, preferred_element_type=jnp.float32))

    row8 = lax.broadcasted_iota(jnp.int32, (8, tm), 0)
    neg = jnp.float32(-jnp.inf)
    gl = jnp.where(row8 < N_GROUPS, lg[N_EXPERTS:N_EXPERTS + 8, :], neg)
    gmax = jnp.max(gl, axis=0, keepdims=True)
    g_top = 1.0 / jnp.sum(jnp.exp(gl - gmax), axis=0, keepdims=True)
    g_idx = jnp.min(jnp.where(gl == gmax, row8, 8), axis=0, keepdims=True)
    e_sel = jnp.zeros((8, tm), jnp.float32)
    for g in range(N_GROUPS):
        e_sel = jnp.where(g_idx == g, lg[g * EXPERTS_PER_GROUP:(g + 1) * EXPERTS_PER_GROUP, :], e_sel)
    e1 = jnp.max(e_sel, axis=0, keepdims=True)
    i1 = jnp.min(jnp.where(e_sel == e1, row8, 8), axis=0, keepdims=True)
    e_rest = jnp.where(row8 == i1, neg, e_sel)
    e2 = jnp.max(e_rest, axis=0, keepdims=True)
    i2 = jnp.min(jnp.where(e_rest == e2, row8, 8), axis=0, keepdims=True)
    t = jnp.exp(e2 - e1)
    w1 = g_top / (1.0 + t)
    w2 = g_top * t / (1.0 + t)
    id1 = g_idx * EXPERTS_PER_GROUP + i1
    id2 = g_idx * EXPERTS_PER_GROUP + i2

    row32 = lax.broadcasted_iota(jnp.int32, (N_EXPERTS, tm), 0)
    oh1 = (row32 == id1).astype(jnp.float32)
    oh2 = (row32 == id2).astype(jnp.float32)
    both = oh1 + oh2
    before = jnp.dot(both.astype(jnp.bfloat16), tri_ref[...], preferred_element_type=jnp.float32)
    base = base_ref[...]
    tot = before + jnp.concatenate([base] * (tm // LANES), axis=1)
    rank1 = jnp.sum(oh1 * tot, axis=0, keepdims=True)
    rank2 = jnp.sum(oh2 * tot, axis=0, keepdims=True)
    new_base = base + jnp.sum(both, axis=1, keepdims=True)
    base_ref[...] = new_base
    cnt_ref[...] = new_base.astype(jnp.int32)

    zi = jnp.zeros((4, tm), jnp.int32)
    ri_ref[...] = jnp.concatenate([id1, id2, rank1.astype(jnp.int32), rank2.astype(jnp.int32), zi], axis=0)
    rf_ref[...] = jnp.concatenate([w1, w2, jnp.zeros((6, tm), jnp.float32)], axis=0)


def _outproj(rows, merged, xp, xs, mod3, g2, wo_bf, wr_hi, wr_lo, tri):
    tm = TM_OUT
    t = rows.t
    return pl.pallas_call(
        functools.partial(_outproj_kernel, rows),
        grid=(t // tm,),
        in_specs=[
            pl.BlockSpec((tm, D_MODEL), lambda i: (i, 0)),
            pl.BlockSpec((tm, D_MODEL), lambda i: (rows.prompt_tile(i, tm), 0)),
            pl.BlockSpec((tm, D_MODEL), lambda i: (rows.sample_tile(i, tm), 0)),
            pl.BlockSpec((1, 6, D_MODEL), lambda i: (rows.seq_of_tile(i, tm), 0, 0)),
            pl.BlockSpec((1, D_MODEL), lambda i: (0, 0)),
            pl.BlockSpec((D_MODEL, D_MODEL), lambda i: (0, 0)),
            pl.BlockSpec((N_EXPERTS + 8, D_MODEL), lambda i: (0, 0)),
            pl.BlockSpec((N_EXPERTS + 8, D_MODEL), lambda i: (0, 0)),
            pl.BlockSpec((tm, tm), lambda i: (0, 0)),
        ],
        out_specs=[
            pl.BlockSpec((tm, D_MODEL), lambda i: (i, 0)),
            pl.BlockSpec((tm, D_MODEL // 2), lambda i: (i, 0)),
            pl.BlockSpec((8, tm), lambda i: (0, i)),
            pl.BlockSpec((8, tm), lambda i: (0, i)),
            pl.BlockSpec((N_EXPERTS, LANES), lambda i: (0, 0)),
        ],
        out_shape=[
            jax.ShapeDtypeStruct((t, D_MODEL), jnp.float32),
            jax.ShapeDtypeStruct((t, D_MODEL // 2), jnp.uint32),
            jax.ShapeDtypeStruct((8, t), jnp.int32),
            jax.ShapeDtypeStruct((8, t), jnp.float32),
            jax.ShapeDtypeStruct((N_EXPERTS, LANES), jnp.int32),
        ],
        scratch_shapes=[pltpu.VMEM((N_EXPERTS, LANES), jnp.float32)],
        compiler_params=_cparams(("arbitrary",)),
        name="outproj",
    )(merged, xp, xs, mod3, g2, wo_bf, wr_hi, wr_lo, tri)


def _dispatch_kernel(pos_ref, h2_ref, init_ref, xs_ref, sem):
    del init_ref
    i = pl.program_id(0)
    t_total = pl.num_programs(0) * TM_DISP
    t0 = i * TM_DISP

    def row_copy(r, dst):
        return pltpu.make_async_copy(h2_ref.at[pl.ds(r, 1), :], xs_ref.at[pl.ds(dst, 1), :], sem)

    def issue(r, carry):
        tok = t0 + r
        row_copy(r, pos_ref[tok]).start()
        row_copy(r, pos_ref[t_total + tok]).start()
        return carry

    lax.fori_loop(0, TM_DISP, issue, 0, unroll=8)

    def drain(r, carry):
        row_copy(0, 0).wait()
        row_copy(0, 0).wait()
        return carry

    lax.fori_loop(0, TM_DISP, drain, 0, unroll=8)


def _dispatch(pos_flat, h2, n_rows):
    t, width = h2.shape
    init = jnp.zeros((n_rows, width), h2.dtype)
    return pl.pallas_call(
        _dispatch_kernel,
        grid_spec=pltpu.PrefetchScalarGridSpec(
            num_scalar_prefetch=1,
            grid=(t // TM_DISP,),
            in_specs=[pl.BlockSpec((TM_DISP, width), lambda i, pos: (i, 0)),
                      pl.BlockSpec(memory_space=pl.ANY)],
            out_specs=pl.BlockSpec(memory_space=pl.ANY),
            scratch_shapes=[pltpu.SemaphoreType.DMA],
        ),
        out_shape=jax.ShapeDtypeStruct((n_rows, width), h2.dtype),
        input_output_aliases={2: 0},
        compiler_params=_cparams(("arbitrary",)),
        name="dispatch",
    )(pos_flat, h2, init)


def _experts_kernel(meta_ref, x_ref, wg_ref, wu_ref, wd_ref, y_ref, wg_s, wu_s, wd_s):
    j = pl.program_id(0)
    n_tiles = pl.num_programs(0)
    active = j < meta_ref[0]

    @pl.when(active & (meta_ref[1 + n_tiles + j] == 1))
    def _():
        wg_s[...] = wg_ref[0].astype(jnp.bfloat16)
        wu_s[...] = wu_ref[0].astype(jnp.bfloat16)
        wd_s[...] = wd_ref[0].astype(jnp.bfloat16)

    @pl.when(active)
    def _():
        xa, xb = _unpack_rows(x_ref[...])
        x = jnp.concatenate([xa.astype(jnp.bfloat16), xb.astype(jnp.bfloat16)], axis=1)
        g = jnp.dot(x, wg_s[...], preferred_element_type=jnp.float32)
        u = jnp.dot(x, wu_s[...], preferred_element_type=jnp.float32)
        hm = (g / (1.0 + jnp.exp(-g)) * u).astype(jnp.bfloat16)
        y_ref[...] = _pack_rows(jnp.dot(hm, wd_s[...], preferred_element_type=jnp.float32))

    @pl.when(jnp.logical_not(active))
    def _():
        y_ref[...] = jnp.zeros(y_ref.shape, jnp.uint32)


def _experts(meta, xs, w_gate, w_up, w_down):
    n_rows = xs.shape[0]
    n_tiles = n_rows // TM_EXP
    row = lambda j, meta: (jnp.minimum(j, meta[0] - 1), 0)
    wsel = lambda j, meta: (meta[1 + j], 0, 0)
    return pl.pallas_call(
        _experts_kernel,
        grid_spec=pltpu.PrefetchScalarGridSpec(
            num_scalar_prefetch=1,
            grid=(n_tiles,),
            in_specs=[pl.BlockSpec((TM_EXP, D_MODEL // 2), row),
                      pl.BlockSpec((1, D_MODEL, D_EXPERT), wsel),
                      pl.BlockSpec((1, D_MODEL, D_EXPERT), wsel),
                      pl.BlockSpec((1, D_EXPERT, D_MODEL), wsel)],
            out_specs=pl.BlockSpec((TM_EXP, D_MODEL // 2), lambda j, meta: (j, 0)),
            scratch_shapes=[pltpu.VMEM((D_MODEL, D_EXPERT), jnp.bfloat16),
                            pltpu.VMEM((D_MODEL, D_EXPERT), jnp.bfloat16),
                            pltpu.VMEM((D_EXPERT, D_MODEL), jnp.bfloat16)],
        ),
        out_shape=jax.ShapeDtypeStruct((n_rows, D_MODEL // 2), jnp.uint32),
        compiler_params=_cparams(("arbitrary",)),
        name="experts",
    )(meta, xs, w_gate, w_up, w_down)


def _combine_kernel(rows, pos_ref, ys_ref, x1_ref, rf_ref, mod_ref, gf_ref, op_ref, os_ref, buf_ref, sem):
    i = pl.program_id(0)
    tm = x1_ref.shape[0]
    t0 = i * tm

    def row_copy(src, slot, r):
        return pltpu.make_async_copy(ys_ref.at[pl.ds(src, 1), :], buf_ref.at[slot, pl.ds(r, 1), :], sem)

    def issue(r, carry):
        row_copy(pos_ref[t0 + r], 0, r).start()
        row_copy(pos_ref[rows.t + t0 + r], 1, r).start()
        return carry

    lax.fori_loop(0, tm, issue, 0, unroll=8)

    def drain(r, carry):
        row_copy(0, 0, 0).wait()
        row_copy(0, 1, 0).wait()
        return carry

    lax.fori_loop(0, tm, drain, 0, unroll=8)

    w = rf_ref[...].T
    y1a, y1b = _unpack_rows(buf_ref[0])
    y2a, y2b = _unpack_rows(buf_ref[1])
    moe = jnp.concatenate([w[:, 0:1] * y1a + w[:, 1:2] * y2a,
                           w[:, 0:1] * y1b + w[:, 1:2] * y2b], axis=1)
    x2 = x1_ref[...] + mod_ref[0][5:6, :] * moe
    out = _rms(x2, gf_ref[...])

    @pl.when(i < rows.np // tm)
    def _():
        op_ref[...] = out

    @pl.when(i >= rows.np // tm)
    def _():
        os_ref[...] = out


def _combine(rows, pos_flat, ys, x1, rf, mod3, gf):
    tm = TM_COMB
    return pl.pallas_call(
        functools.partial(_combine_kernel, rows),
        grid_spec=pltpu.PrefetchScalarGridSpec(
            num_scalar_prefetch=1,
            grid=(rows.t // tm,),
            in_specs=[
                pl.BlockSpec(memory_space=pl.ANY),
                pl.BlockSpec((tm, D_MODEL), lambda i, pos: (i, 0)),
                pl.BlockSpec((8, tm), lambda i, pos: (0, i)),
                pl.BlockSpec((1, 6, D_MODEL), lambda i, pos: (rows.seq_of_tile(i, tm), 0, 0)),
                pl.BlockSpec((1, D_MODEL), lambda i, pos: (0, 0)),
            ],
            out_specs=[
                pl.BlockSpec((tm, D_MODEL), lambda i, pos: (rows.prompt_tile(i, tm), 0)),
                pl.BlockSpec((tm, D_MODEL), lambda i, pos: (rows.sample_tile(i, tm), 0)),
            ],
            scratch_shapes=[pltpu.VMEM((2, tm, D_MODEL // 2), jnp.uint32), pltpu.SemaphoreType.DMA],
        ),
        out_shape=[jax.ShapeDtypeStruct((rows.np, D_MODEL), jnp.float32),
                   jax.ShapeDtypeStruct((rows.t - rows.np, D_MODEL), jnp.float32)],
        compiler_params=_cparams(("arbitrary",)),
        name="combine",
    )(pos_flat, ys, x1, rf, mod3, gf)


def _rope_tables(seq_len):
    pos = jnp.arange(seq_len, dtype=jnp.float32)
    inv = 1.0 / (ROPE_THETA ** (jnp.arange(0, HEAD_DIM, 2, dtype=jnp.float32) / HEAD_DIM))
    ang = pos[:, None] * inv[None, :]
    cos, sin = jnp.cos(ang), jnp.sin(ang)
    cos_t = jnp.tile(cos, (1, LANES // (HEAD_DIM // 2)))
    sin_t = jnp.tile(jnp.concatenate([-sin, sin], axis=1), (1, LANES // HEAD_DIM))
    return cos_t, sin_t


def _routing_tables(ri, cnt, n_tiles_max):
    counts = cnt[:, 0]
    padded = (counts + TM_EXP - 1) // TM_EXP * TM_EXP
    seg_end = jnp.cumsum(padded)
    seg_start = seg_end - padded
    pos1 = seg_start[ri[0]] + ri[2]
    pos2 = seg_start[ri[1]] + ri[3]
    n_active = seg_end[-1] // TM_EXP
    tile_row = jnp.minimum(jnp.arange(n_tiles_max, dtype=jnp.int32), n_active - 1) * TM_EXP
    tile_expert = jnp.sum((tile_row[:, None] >= seg_end[None, :]).astype(jnp.int32), axis=1)
    first = jnp.any(tile_row[:, None] == seg_start[None, :], axis=1) & (padded[tile_expert] > 0)
    meta = jnp.concatenate([n_active[None].astype(jnp.int32), tile_expert.astype(jnp.int32),
                            first.astype(jnp.int32)])
    return jnp.concatenate([pos1, pos2]).astype(jnp.int32), meta


def kernel(x_prompt, x_sample, c_prompt, c_sample, w_ada, b_ada, norm1_g, w_in, lambda_q1, lambda_k1,
           lambda_q2, lambda_k2, subln_g, gmlp_norm_g, w_spatial, b_spatial, w_branch_att,
           w_branch_gmlp, w_out, norm2_g, w_router_group, w_router_expert, w_gate, w_up, w_down,
           final_norm_g):
    bf = jnp.bfloat16
    bp, sp, d = x_prompt.shape
    bs, ss, _ = x_sample.shape
    assert bp == 1 and d == D_MODEL
    rows = _Rows(bp * sp, bs, ss)
    xp = x_prompt.reshape(bp * sp, d)
    xs = x_sample.reshape(bs * ss, d)

    c8 = jnp.concatenate([c_prompt, c_sample, jnp.zeros((8 - bp - bs, d), jnp.float32)], axis=0)
    mod3 = _ada(c8, w_ada[0], b_ada).reshape(8, 6, d)

    cos_t, sin_t = _rope_tables(max(sp, ss))
    z = _inproj(rows, xp, xs, mod3, norm1_g, w_in[0].astype(bf), cos_t, sin_t, gmlp_norm_g)

    lam_p = jnp.concatenate([lambda_q1, lambda_k1, lambda_q2, lambda_k2], axis=0)
    a_p = _attn(z, lam_p, subln_g, bp, sp, 0)
    a_s = _attn(z, lam_p, subln_g, bs, ss, bp * sp)

    bias_tile = jnp.repeat(b_spatial[0].T, GMLP_GROUP_DIM, axis=1)
    merged = _mix(rows, a_p, a_s, z, w_spatial[0].astype(bf), bias_tile,
                  w_branch_att[0].astype(bf), w_branch_gmlp[0].astype(bf))

    wr_t = jnp.concatenate([w_router_expert[0].T, w_router_group[0].T,
                            jnp.zeros((8 - N_GROUPS, d), jnp.float32)], axis=0)
    wr_hi = wr_t.astype(bf)
    wr_lo = (wr_t - wr_hi.astype(jnp.float32)).astype(bf)
    tri = jnp.triu(jnp.ones((TM_OUT, TM_OUT), bf), k=1)
    x1, h2, ri, rf, cnt = _outproj(rows, merged, xp, xs, mod3, norm2_g, w_out[0].astype(bf),
                                   wr_hi, wr_lo, tri)

    n_rows = (2 * rows.t // TM_EXP + N_EXPERTS) * TM_EXP
    pos_flat, meta = _routing_tables(ri, cnt, n_rows // TM_EXP)
    xs_sorted = _dispatch(pos_flat, h2, n_rows)
    ys = _experts(meta, xs_sorted, w_gate[0], w_up[0], w_down[0])
    y_p, y_s = _combine(rows, pos_flat, ys, x1, rf, mod3, final_norm_g.reshape(1, d))
    return y_p.reshape(bp, sp, d), y_s.reshape(bs, ss, d)
```

```python
import functools
import math

import jax
import jax.numpy as jnp
from jax import lax
from jax.experimental import pallas as pl
from jax.experimental.pallas import tpu as pltpu

D_MODEL = 2048
N_HEADS = 8
HEAD_DIM = 64
V_DIM = 2 * HEAD_DIM
QK_WIDTH = N_HEADS * 2 * HEAD_DIM
V_WIDTH = N_HEADS * V_DIM
ROPE_THETA = 10000.0
GMLP_GROUPS = 8
GMLP_GROUP_DIM = 128
GMLP_WIDTH = GMLP_GROUPS * GMLP_GROUP_DIM
GMLP_CHUNK = 128
IN_COLS = 3 * 1024 + 2 * 1024 + 2 * D_MODEL
N_GROUPS = 4
EXPERTS_PER_GROUP = 8
N_EXPERTS = N_GROUPS * EXPERTS_PER_GROUP
D_EXPERT = 512
EPS = 1e-6
LAMBDA_INIT = 0.8 - 0.6 * math.exp(-0.3 * 0)

LANES = 128
COL_TILE = 1024
N_COL_TILES = IN_COLS // COL_TILE
VMEM_LIMIT = 56 * 1024 * 1024

TM_IN = 512
TM_MIX = 512
TM_OUT = 256
TQ = 512
TK = 1024
TM_EXP = 256
TM_DISP = 512
TM_COMB = 256


def _cparams(sem, flags=None):
    return pltpu.CompilerParams(dimension_semantics=sem, vmem_limit_bytes=VMEM_LIMIT, flags=flags)


def _rms(x, g):
    return x * lax.rsqrt(jnp.mean(x * x, axis=-1, keepdims=True) + EPS) * g


def _pack_rows(x):
    n = x.shape[1] // 2
    hi = lax.bitcast_convert_type(x[:, :n].astype(jnp.bfloat16).astype(jnp.float32), jnp.uint32)
    lo = lax.bitcast_convert_type(x[:, n:].astype(jnp.bfloat16).astype(jnp.float32), jnp.uint32)
    return hi | (lo >> 16)


def _unpack_rows(w):
    a = lax.bitcast_convert_type(w & jnp.uint32(0xFFFF0000), jnp.float32)
    b = lax.bitcast_convert_type(w << 16, jnp.float32)
    return a, b


def _ada_kernel(c_ref, w_ref, b_ref, o_ref):
    c = c_ref[...]
    s = (c / (1.0 + jnp.exp(-c))).astype(jnp.bfloat16)
    o_ref[...] = jnp.dot(s, w_ref[...].astype(jnp.bfloat16),
                         preferred_element_type=jnp.float32) + b_ref[...]


def _ada(c8, w_ada, b_ada):
    n = w_ada.shape[1]
    tn = 1024
    return pl.pallas_call(
        _ada_kernel,
        grid=(n // tn,),
        in_specs=[pl.BlockSpec((8, D_MODEL), lambda j: (0, 0)),
                  pl.BlockSpec((D_MODEL, tn), lambda j: (0, j)),
                  pl.BlockSpec((1, tn), lambda j: (0, j))],
        out_specs=pl.BlockSpec((8, tn), lambda j: (0, j)),
        out_shape=jax.ShapeDtypeStruct((8, n), jnp.float32),
        compiler_params=_cparams(("arbitrary",)),
        name="ada",
    )(c8, w_ada, b_ada)


class _Rows:
    def __init__(self, n_prompt, n_sample_seq, sample_len):
        self.np = n_prompt
        self.ns = n_sample_seq
        self.ls = sample_len
        self.t = n_prompt + n_sample_seq * sample_len

    def seq_of_tile(self, i, tm):
        npt = self.np // tm
        return jnp.where(i < npt, 0, 1 + (i - npt) // (self.ls // tm))

    def pos_tile(self, i, tm):
        npt = self.np // tm
        return jnp.where(i < npt, i, (i - npt) % (self.ls // tm))

    def prompt_tile(self, i, tm):
        return jnp.minimum(i, self.np // tm - 1)

    def sample_tile(self, i, tm):
        return jnp.maximum(i - self.np // tm, 0)


def _inproj_kernel(rows, xp_ref, xs_ref, mod_ref, g1_ref, w_ref, cos_ref, sin_ref, gn_ref,
                   z_ref, h_ref):
    i = pl.program_id(0)
    j = pl.program_id(1)
    tm = h_ref.shape[0]

    @pl.when(j == 0)
    def _():
        m = mod_ref[0]
        x = jnp.where(i < rows.np // tm, xp_ref[...], xs_ref[...])
        h = _rms(x, g1_ref[...]) * (1.0 + m[1:2, :]) + m[0:1, :]
        h_ref[...] = h.astype(jnp.bfloat16)

    def zdot():
        return jnp.dot(h_ref[...], w_ref[...], preferred_element_type=jnp.float32)

    def rope(scale):
        z = zdot()
        c = cos_ref[...]
        s = sin_ref[...]
        lane = lax.broadcasted_iota(jnp.int32, (tm, LANES), 1)
        first_half = (lane % HEAD_DIM) < (HEAD_DIM // 2)
        for b in range(COL_TILE // LANES):
            zb = z[:, b * LANES:(b + 1) * LANES]
            up = pltpu.roll(zb, LANES - HEAD_DIM // 2, 1)
            dn = pltpu.roll(zb, HEAD_DIM // 2, 1)
            r = (zb * c + jnp.where(first_half, up, dn) * s) * scale
            z_ref[:, b * LANES:(b + 1) * LANES] = r.astype(jnp.bfloat16)

    @pl.when(j == 0)
    def _():
        rope(HEAD_DIM ** -0.5 * math.log2(math.e))

    @pl.when(j == 1)
    def _():
        rope(1.0)

    @pl.when(j == 2)
    def _():
        z_ref[...] = zdot().astype(jnp.bfloat16)

    @pl.when(j == 3)
    def _():
        z_ref[...] = jax.nn.gelu(zdot(), approximate=True).astype(jnp.bfloat16)

    @pl.when(j == 4)
    def _():
        z_ref[...] = _rms(jax.nn.gelu(zdot(), approximate=True), gn_ref[...]).astype(jnp.bfloat16)

    @pl.when(j >= 5)
    def _():
        z_ref[...] = (1.0 / (1.0 + jnp.exp(-zdot()))).astype(jnp.bfloat16)


def _inproj(rows, xp, xs, mod3, g1, w_in_bf, cos_t, sin_t, gn):
    tm = TM_IN
    return pl.pallas_call(
        functools.partial(_inproj_kernel, rows),
        grid=(rows.t // tm, N_COL_TILES),
        in_specs=[
            pl.BlockSpec((tm, D_MODEL), lambda i, j: (rows.prompt_tile(i, tm), 0)),
            pl.BlockSpec((tm, D_MODEL), lambda i, j: (rows.sample_tile(i, tm), 0)),
            pl.BlockSpec((1, 6, D_MODEL), lambda i, j: (rows.seq_of_tile(i, tm), 0, 0)),
            pl.BlockSpec((1, D_MODEL), lambda i, j: (0, 0)),
            pl.BlockSpec((D_MODEL, COL_TILE), lambda i, j: (0, j)),
            pl.BlockSpec((tm, LANES), lambda i, j: (rows.pos_tile(i, tm), 0)),
            pl.BlockSpec((tm, LANES), lambda i, j: (rows.pos_tile(i, tm), 0)),
            pl.BlockSpec((1, COL_TILE), lambda i, j: (0, 0)),
        ],
        out_specs=pl.BlockSpec((tm, COL_TILE), lambda i, j: (i, j)),
        out_shape=jax.ShapeDtypeStruct((rows.t, IN_COLS), jnp.bfloat16),
        scratch_shapes=[pltpu.VMEM((tm, D_MODEL), jnp.bfloat16)],
        compiler_params=_cparams(("arbitrary", "arbitrary")),
        name="inproj",
    )(xp, xs, mod3, g1, w_in_bf, cos_t, sin_t, gn)


ONES_ROWS = 16


def _attn_kernel(n_k, q_ref, k_ref, v_ref, lam_ref, sg_ref, o_ref, qq_ref, vt_ref, acc_ref, m_ref,
                 sa_ref, sb_ref):
    tq = q_ref.shape[0]

    @pl.when(pl.program_id(2) == 0)
    def _():
        vt_ref[V_DIM:, :] = jnp.ones((ONES_ROWS, vt_ref.shape[1]), jnp.bfloat16)

        def transpose_chunk(c, carry):
            off = pl.multiple_of(c * TK, TK)
            vc = v_ref[pl.ds(off, TK), :].astype(jnp.float32)
            vt_ref[0:V_DIM, pl.ds(off, TK)] = vc.T.astype(jnp.bfloat16)
            return carry

        lax.fori_loop(0, n_k, transpose_chunk, 0)

    q = q_ref[...]
    lane = lax.broadcasted_iota(jnp.int32, q.shape, 1)
    zero = jnp.zeros_like(q)
    qq_ref[0:tq, :] = jnp.where(lane < HEAD_DIM, q, zero)
    qq_ref[tq:2 * tq, :] = jnp.where(lane >= HEAD_DIM, q, zero)
    m_ref[...] = jnp.full(m_ref.shape, -jnp.inf, jnp.float32)
    acc_ref[...] = jnp.zeros(acc_ref.shape, jnp.float32)

    def scores(ik, s_ref):
        off = pl.multiple_of(ik * TK, TK)
        s_ref[...] = lax.dot_general(k_ref[pl.ds(off, TK), :], qq_ref[...], (((1,), (1,)), ((), ())),
                                     preferred_element_type=jnp.float32)

    def accumulate(ik, s_ref):
        off = pl.multiple_of(ik * TK, TK)
        s = s_ref[...]
        m_old = m_ref[...]
        m_new = jnp.maximum(m_old, jnp.max(s, axis=0, keepdims=True))
        p = jnp.exp2(s - m_new).astype(jnp.bfloat16)
        alpha = jnp.exp2(m_old - m_new)
        m_ref[...] = m_new
        pv = jnp.dot(vt_ref[:, pl.ds(off, TK)], p, preferred_element_type=jnp.float32)
        acc_ref[...] = alpha * acc_ref[...] + pv

    def chunk_pair(ik, last):
        scores(ik + 1, sb_ref)
        accumulate(ik, sa_ref)
        if not last:
            scores(ik + 2, sa_ref)
        accumulate(ik + 1, sb_ref)

    scores(0, sa_ref)

    def body(i, carry):
        chunk_pair(2 * i, False)
        return carry

    lax.fori_loop(0, n_k // 2 - 1, body, 0)
    chunk_pair(n_k - 2, True)

    lp = lam_ref[...]
    lam = (jnp.exp(jnp.sum(lp[0:1, :] * lp[1:2, :], axis=1, keepdims=True))
           - jnp.exp(jnp.sum(lp[2:3, :] * lp[3:4, :], axis=1, keepdims=True)) + LAMBDA_INIT)
    acc = acc_ref[0:V_DIM, :]
    l = acc_ref[V_DIM:V_DIM + 1, :]
    o_t =acc[:, 0:tq] / l[:, 0:tq] - lam * (acc[:, tq:2 * tq] / l[:, tq:2 * tq])
    o = o_t.T
    o_ref[...] = (_rms(o, sg_ref[...]) * (1.0 - LAMBDA_INIT)).astype(jnp.bfloat16)


def _attn(z, lam_p, subln_g, n_seq, seq_len, row0):
    tq = min(TQ, seq_len)
    nq = seq_len // tq
    assert seq_len % (2 * TK) == 0 and row0 % seq_len == 0
    qb0 = row0 // tq
    sb0 = row0 // seq_len
    kern = functools.partial(_attn_kernel, seq_len // TK)
    return pl.pallas_call(
        kern,
        grid=(n_seq, N_HEADS, nq),
        in_specs=[
            pl.BlockSpec((tq, LANES), lambda b, h, iq: (qb0 + b * nq + iq, h)),
            pl.BlockSpec((seq_len, LANES), lambda b, h, iq: (sb0 + b, N_HEADS + h)),
            pl.BlockSpec((seq_len, LANES), lambda b, h, iq: (sb0 + b, 2 * N_HEADS + h)),
            pl.BlockSpec((4, HEAD_DIM), lambda b, h, iq: (0, 0)),
            pl.BlockSpec((1, V_DIM), lambda b, h, iq: (0, 0)),
        ],
        out_specs=pl.BlockSpec((tq, LANES), lambda b, h, iq: (b * nq + iq, h)),
        out_shape=jax.ShapeDtypeStruct((n_seq * seq_len, V_WIDTH), jnp.bfloat16),
        scratch_shapes=[pltpu.VMEM((2 * tq, LANES), jnp.bfloat16),
                        pltpu.VMEM((V_DIM + ONES_ROWS, seq_len), jnp.bfloat16),
                        pltpu.VMEM((V_DIM + ONES_ROWS, 2 * tq), jnp.float32),
                        pltpu.VMEM((1, 2 * tq), jnp.float32),
                        pltpu.VMEM((TK, 2 * tq), jnp.float32),
                        pltpu.VMEM((TK, 2 * tq), jnp.float32)],
        compiler_params=_cparams(("arbitrary", "arbitrary", "arbitrary")),
        name="attn",
    )(z, z, z, lam_p, subln_g)


def _mix_kernel(ap_ref, as_ref, gu_ref, gv_ref, sa0_ref, sa1_ref, sb0_ref, sb1_ref,
                ws_ref, bs_ref, wba_ref, wbg_ref, o_ref, m_ref, *, n_prompt_tiles):
    i = pl.program_id(0)
    tm = gu_ref.shape[0]
    bias = bs_ref[...]
    for c in range(tm // GMLP_CHUNK):
        r0 = c * GMLP_CHUNK
        for g in range(GMLP_GROUPS):
            c0 = g * GMLP_GROUP_DIM
            mixed = jnp.dot(ws_ref[g], gv_ref[r0:r0 + GMLP_CHUNK, c0:c0 + GMLP_GROUP_DIM],
                            preferred_element_type=jnp.float32) + bias[:, c0:c0 + GMLP_GROUP_DIM]
            gu = gu_ref[r0:r0 + GMLP_CHUNK, c0:c0 + GMLP_GROUP_DIM].astype(jnp.float32)
            m_ref[r0:r0 + GMLP_CHUNK, c0:c0 + GMLP_GROUP_DIM] = (gu * mixed).astype(jnp.bfloat16)

    a = jnp.where(i < n_prompt_tiles, ap_ref[...], as_ref[...])
    m = m_ref[...]
    for nb, (sa_ref, sb_ref) in enumerate(((sa0_ref, sb0_ref), (sa1_ref, sb1_ref))):
        cs = slice(nb * COL_TILE, (nb + 1) * COL_TILE)
        pa = jnp.dot(a, wba_ref[:, cs], preferred_element_type=jnp.float32)
        pb = jnp.dot(m, wbg_ref[:, cs], preferred_element_type=jnp.float32)
        merged = sa_ref[...].astype(jnp.float32) * pa + sb_ref[...].astype(jnp.float32) * pb
        o_ref[:, cs] = merged.astype(jnp.bfloat16)


def _mix(rows, a_p, a_s, z, ws_bf, bias_tile, wba_bf, wbg_bf):
    tm = TM_MIX
    zcol = lambda cb: pl.BlockSpec((tm, COL_TILE), lambda i: (i, cb))
    return pl.pallas_call(
        functools.partial(_mix_kernel, n_prompt_tiles=rows.np // tm),
        grid=(rows.t // tm,),
        in_specs=[
            pl.BlockSpec((tm, V_WIDTH), lambda i: (rows.prompt_tile(i, tm), 0)),
            pl.BlockSpec((tm, V_WIDTH), lambda i: (rows.sample_tile(i, tm), 0)),
            zcol(3), zcol(4), zcol(5), zcol(6), zcol(7), zcol(8),
            pl.BlockSpec((GMLP_GROUPS, GMLP_CHUNK, GMLP_CHUNK), lambda i: (0, 0, 0)),
            pl.BlockSpec((GMLP_CHUNK, GMLP_WIDTH), lambda i: (0, 0)),
            pl.BlockSpec((V_WIDTH, D_MODEL), lambda i: (0, 0)),
            pl.BlockSpec((GMLP_WIDTH, D_MODEL), lambda i: (0, 0)),
        ],
        out_specs=pl.BlockSpec((tm, D_MODEL), lambda i: (i, 0)),
        out_shape=jax.ShapeDtypeStruct((rows.t, D_MODEL), jnp.bfloat16),
        scratch_shapes=[pltpu.VMEM((tm, GMLP_WIDTH), jnp.bfloat16)],
        compiler_params=_cparams(("arbitrary",)),
        name="mix",
    )(a_p, a_s, z, z, z, z, z, z, ws_bf, bias_tile, wba_bf, wbg_bf)


def _outproj_kernel(rows, mg_ref, xp_ref, xs_ref, mod_ref, g2_ref, wo_ref, wr_ref, tri_ref,
                    x1_ref, h2_ref, ri_ref, rf_ref, cnt_ref, base_ref):
    i = pl.program_id(0)
    tm = mg_ref.shape[0]

    @pl.when(i == 0)
    def _():
        base_ref[...] = jnp.zeros(base_ref.shape, jnp.float32)

    m = mod_ref[0]
    y = jnp.dot(mg_ref[...], wo_ref[...], preferred_element_type=jnp.float32)

    x = jnp.where(i < rows.np // tm, xp_ref[...], xs_ref[...])
    x1 = x + m[2:3, :] * y
    x1_ref[...] = x1
    h2 = _rms(x1, g2_ref[...]) * (1.0 + m[4:5, :]) + m[3:4, :]
    h2_ref[...] = _pack_rows(h2)

    h_hi = h2.astype(jnp.bfloat16)
    h_lo = (h2 - h_hi.astype(jnp.float32)).astype(jnp.bfloat16)
    hh = jnp.dot(h_hi, wr_ref[...], preferred_element_type=jnp.float32)
    lh = jnp.dot(h_lo, wr_ref[:, 0:LANES], preferred_element_type=jnp.float32)
    lg = (hh[:, 0:LANES] + hh[:, LANES:2 * LANES] + lh).T

    row8 = lax.broadcasted_iota(jnp.int32, (8, tm), 0)
    neg = jnp.float32(-jnp.inf)
    gl = jnp.where(row8 < N_GROUPS, lg[N_EXPERTS:N_EXPERTS + 8, :], neg)
    gmax = jnp.max(gl, axis=0, keepdims=True)
    g_top = 1.0 / jnp.sum(jnp.exp(gl - gmax), axis=0, keepdims=True)
    g_idx = jnp.min(jnp.where(gl == gmax, row8, 8), axis=0, keepdims=True)
    e_sel = jnp.zeros((8, tm), jnp.float32)
    for g in range(N_GROUPS):
        e_sel = jnp.where(g_idx == g, lg[g * EXPERTS_PER_GROUP:(g + 1) * EXPERTS_PER_GROUP, :], e_sel)
    e1 = jnp.max(e_sel, axis=0, keepdims=True)
    i1 = jnp.min(jnp.where(e_sel == e1, row8, 8), axis=0, keepdims=True)
    e_rest = jnp.where(row8 == i1, neg, e_sel)
    e2 = jnp.max(e_rest, axis=0, keepdims=True)
    i2 = jnp.min(jnp.where(e_rest == e2, row8, 8), axis=0, keepdims=True)
    t = jnp.exp(e2 - e1)
    w1 = g_top / (1.0 + t)
    w2 = g_top * t / (1.0 + t)
    id1 = g_idx * EXPERTS_PER_GROUP + i1
    id2 = g_idx * EXPERTS_PER_GROUP + i2

    row32 = lax.broadcasted_iota(jnp.int32, (N_EXPERTS, tm), 0)
    oh1 = (row32 == id1).astype(jnp.float32)
    oh2 = (row32 == id2).astype(jnp.float32)
    both = oh1 + oh2
    before = jnp.dot(both.astype(jnp.bfloat16), tri_ref[...], preferred_element_type=jnp.float32)
    base = base_ref[...]
    tot = before + jnp.concatenate([base] * (tm // LANES), axis=1)
    rank1 = jnp.sum(oh1 * tot, axis=0, keepdims=True)
    rank2 = jnp.sum(oh2 * tot, axis=0, keepdims=True)
    new_base = base + jnp.sum(both, axis=1, keepdims=True)
    base_ref[...] = new_base
    cnt_ref[...] = new_base.astype(jnp.int32)

    zi = jnp.zeros((4, tm), jnp.int32)
    ri_ref[...] = jnp.concatenate([id1, id2, rank1.astype(jnp.int32), rank2.astype(jnp.int32), zi], axis=0)
    rf_ref[...] = jnp.concatenate([w1, w2, jnp.zeros((6, tm), jnp.float32)], axis=0)


def _outproj(rows, merged, xp, xs, mod3, g2, wo_bf, wr_cat, tri):
    tm = TM_OUT
    t = rows.t
    return pl.pallas_call(
        functools.partial(_outproj_kernel, rows),
        grid=(t // tm,),
        in_specs=[
            pl.BlockSpec((tm, D_MODEL), lambda i: (i, 0)),
            pl.BlockSpec((tm, D_MODEL), lambda i: (rows.prompt_tile(i, tm), 0)),
            pl.BlockSpec((tm, D_MODEL), lambda i: (rows.sample_tile(i, tm), 0)),
            pl.BlockSpec((1, 6, D_MODEL), lambda i: (rows.seq_of_tile(i, tm), 0, 0)),
            pl.BlockSpec((1, D_MODEL), lambda i: (0, 0)),
            pl.BlockSpec((D_MODEL, D_MODEL), lambda i: (0, 0)),
            pl.BlockSpec((D_MODEL, 2 * LANES), lambda i: (0, 0)),
            pl.BlockSpec((tm, tm), lambda i: (0, 0)),
        ],
        out_specs=[
            pl.BlockSpec((tm, D_MODEL), lambda i: (i, 0)),
            pl.BlockSpec((tm, D_MODEL // 2), lambda i: (i, 0)),
            pl.BlockSpec((8, tm), lambda i: (0, i)),
            pl.BlockSpec((8, tm), lambda i: (0, i)),
            pl.BlockSpec((N_EXPERTS, LANES), lambda i: (0, 0)),
        ],
        out_shape=[
            jax.ShapeDtypeStruct((t, D_MODEL), jnp.float32),
            jax.ShapeDtypeStruct((t, D_MODEL // 2), jnp.uint32),
            jax.ShapeDtypeStruct((8, t), jnp.int32),
            jax.ShapeDtypeStruct((8, t), jnp.float32),
            jax.ShapeDtypeStruct((N_EXPERTS, LANES), jnp.int32),
        ],
        scratch_shapes=[pltpu.VMEM((N_EXPERTS, LANES), jnp.float32)],
        compiler_params=_cparams(("arbitrary",)),
        name="outproj",
    )(merged, xp, xs, mod3, g2, wo_bf, wr_cat, tri)


def _dispatch_kernel(pos_ref, h2_ref, init_ref, xs_ref, sem):
    del init_ref
    i = pl.program_id(0)
    t_total = pl.num_programs(0) * TM_DISP
    t0 = i * TM_DISP

    def row_copy(r, dst):
        return pltpu.make_async_copy(h2_ref.at[pl.ds(r, 1), :], xs_ref.at[pl.ds(dst, 1), :], sem)

    for r in range(TM_DISP):
        row_copy(r, pos_ref[t0 + r]).start()
        row_copy(r, pos_ref[t_total + t0 + r]).start()

    for _ in range(2 * TM_DISP):
        row_copy(0, 0).wait()


def _dispatch(pos_flat, h2, n_rows):
    t, width = h2.shape
    init = jnp.zeros((n_rows, width), h2.dtype)
    return pl.pallas_call(
        _dispatch_kernel,
        grid_spec=pltpu.PrefetchScalarGridSpec(
            num_scalar_prefetch=1,
            grid=(t // TM_DISP,),
            in_specs=[pl.BlockSpec((TM_DISP, width), lambda i, pos: (i, 0)),
                      pl.BlockSpec(memory_space=pl.ANY)],
            out_specs=pl.BlockSpec(memory_space=pl.ANY),
            scratch_shapes=[pltpu.SemaphoreType.DMA],
        ),
        out_shape=jax.ShapeDtypeStruct((n_rows, width), h2.dtype),
        input_output_aliases={2: 0},
        compiler_params=_cparams(("arbitrary",)),
        name="dispatch",
    )(pos_flat, h2, init)


def _experts_kernel(meta_ref, x_ref, wg_ref, wu_ref, wd_ref, y_ref, wg_s, wu_s, wd_s):
    j = pl.program_id(0)
    n_tiles = pl.num_programs(0)
    active = j < meta_ref[0]

    @pl.when(active & (meta_ref[1 + n_tiles + j] == 1))
    def _():
        wg_s[...] = wg_ref[0].astype(jnp.bfloat16)
        wu_s[...] = wu_ref[0].astype(jnp.bfloat16)
        wd_s[...] = wd_ref[0].astype(jnp.bfloat16)

    @pl.when(active)
    def _():
        xa, xb = _unpack_rows(x_ref[...])
        x = jnp.concatenate([xa.astype(jnp.bfloat16), xb.astype(jnp.bfloat16)], axis=1)
        g = jnp.dot(x, wg_s[...], preferred_element_type=jnp.float32)
        u = jnp.dot(x, wu_s[...], preferred_element_type=jnp.float32)
        hm = (g / (1.0 + jnp.exp(-g)) * u).astype(jnp.bfloat16)
        y_ref[...] = _pack_rows(jnp.dot(hm, wd_s[...], preferred_element_type=jnp.float32))

    @pl.when(jnp.logical_not(active))
    def _():
        y_ref[...] = jnp.zeros(y_ref.shape, jnp.uint32)


def _experts(meta, xs, w_gate, w_up, w_down):
    n_rows = xs.shape[0]
    n_tiles = n_rows // TM_EXP
    row = lambda j, meta: (jnp.minimum(j, meta[0] - 1), 0)
    wsel = lambda j, meta: (meta[1 + j], 0, 0)
    return pl.pallas_call(
        _experts_kernel,
        grid_spec=pltpu.PrefetchScalarGridSpec(
            num_scalar_prefetch=1,
            grid=(n_tiles,),
            in_specs=[pl.BlockSpec((TM_EXP, D_MODEL // 2), row),
                      pl.BlockSpec((1, D_MODEL, D_EXPERT), wsel),
                      pl.BlockSpec((1, D_MODEL, D_EXPERT), wsel),
                      pl.BlockSpec((1, D_EXPERT, D_MODEL), wsel)],
            out_specs=pl.BlockSpec((TM_EXP, D_MODEL // 2), lambda j, meta: (j, 0)),
            scratch_shapes=[pltpu.VMEM((D_MODEL, D_EXPERT), jnp.bfloat16),
                            pltpu.VMEM((D_MODEL, D_EXPERT), jnp.bfloat16),
                            pltpu.VMEM((D_EXPERT, D_MODEL), jnp.bfloat16)],
        ),
        out_shape=jax.ShapeDtypeStruct((n_rows, D_MODEL // 2), jnp.uint32),
        compiler_params=_cparams(("arbitrary",)),
        name="experts",
    )(meta, xs, w_gate, w_up, w_down)


COMB_SETS = 3
COMB_CHUNK = 32


def _combine_kernel(rows, pos_ref, ys_ref, x1_ref, rf_ref, mod_ref, gf_ref, op_ref, os_ref,
                    buf_ref, out_ref, sems):
    i = pl.program_id(0)
    n = pl.num_programs(0)
    tm = x1_ref.shape[0]

    def row_copy(src, st, slot, r):
        return pltpu.make_async_copy(ys_ref.at[pl.ds(src, 1), :],
                                     buf_ref.at[2 * st + slot, pl.ds(r, 1), :], sems.at[st])

    def fetch_row(tile, st, r):
        t0 = jnp.minimum(tile, n - 1) * tm
        row_copy(pos_ref[t0 + r], st, 0, r).start()
        row_copy(pos_ref[rows.t + t0 + r], st, 1, r).start()

    def drain(st):
        for _ in range(2 * tm):
            row_copy(0, st, 0, 0).wait()

    @pl.when(i == 0)
    def _():
        def first_two(r, carry):
            fetch_row(0, 0, r)
            fetch_row(1, 1, r)
            return carry

        lax.fori_loop(0, tm, first_two, 0, unroll=8)

    cur = lax.rem(i, COMB_SETS)
    nxt = lax.rem(i + 2, COMB_SETS)
    drain(cur)

    w = rf_ref[...].T
    g2 = mod_ref[0][5:6, :]
    gf = gf_ref[...]
    for c in range(tm // COMB_CHUNK):
        rs = slice(c * COMB_CHUNK, (c + 1) * COMB_CHUNK)
        y1a, y1b = _unpack_rows(buf_ref[2 * cur, rs, :])
        y2a, y2b = _unpack_rows(buf_ref[2 * cur + 1, rs, :])
        w1 = w[rs, 0:1]
        w2 = w[rs, 1:2]
        moe = jnp.concatenate([w1 * y1a + w2 * y2a, w1 * y1b + w2 * y2b], axis=1)
        out_ref[rs, :] = _rms(x1_ref[rs, :] + g2 * moe, gf)
        for r in range(c * COMB_CHUNK, (c + 1) * COMB_CHUNK):
            fetch_row(i + 2, nxt, r)

    @pl.when(i < rows.np // tm)
    def _():
        op_ref[...] = out_ref[...]

    @pl.when(i >= rows.np // tm)
    def _():
        os_ref[...] = out_ref[...]

    @pl.when(i == n - 1)
    def _():
        drain(lax.rem(i + 1, COMB_SETS))
        drain(nxt)


def _combine(rows, pos_flat, ys, x1, rf, mod3, gf):
    tm = TM_COMB
    return pl.pallas_call(
        functools.partial(_combine_kernel, rows),
        grid_spec=pltpu.PrefetchScalarGridSpec(
            num_scalar_prefetch=1,
            grid=(rows.t // tm,),
            in_specs=[
                pl.BlockSpec(memory_space=pl.ANY),
                pl.BlockSpec((tm, D_MODEL), lambda i, pos: (i, 0)),
                pl.BlockSpec((8, tm), lambda i, pos: (0, i)),
                pl.BlockSpec((1, 6, D_MODEL), lambda i, pos: (rows.seq_of_tile(i, tm), 0, 0)),
                pl.BlockSpec((1, D_MODEL), lambda i, pos: (0, 0)),
            ],
            out_specs=[
                pl.BlockSpec((tm, D_MODEL), lambda i, pos: (rows.prompt_tile(i, tm), 0)),
                pl.BlockSpec((tm, D_MODEL), lambda i, pos: (rows.sample_tile(i, tm), 0)),
            ],
            scratch_shapes=[pltpu.VMEM((2 * COMB_SETS, tm, D_MODEL // 2), jnp.uint32),
                            pltpu.VMEM((tm, D_MODEL), jnp.float32),
                            pltpu.SemaphoreType.DMA((COMB_SETS,))],
        ),
        out_shape=[jax.ShapeDtypeStruct((rows.np, D_MODEL), jnp.float32),
                   jax.ShapeDtypeStruct((rows.t - rows.np, D_MODEL), jnp.float32)],
        compiler_params=_cparams(("arbitrary",)),
        name="combine",
    )(pos_flat, ys, x1, rf, mod3, gf)


def _rope_tables(seq_len):
    pos = jnp.arange(seq_len, dtype=jnp.float32)
    inv = 1.0 / (ROPE_THETA ** (jnp.arange(0, HEAD_DIM, 2, dtype=jnp.float32) / HEAD_DIM))
    ang = pos[:, None] * inv[None, :]
    cos, sin = jnp.cos(ang), jnp.sin(ang)
    cos_t = jnp.tile(cos, (1, LANES // (HEAD_DIM // 2)))
    sin_t = jnp.tile(jnp.concatenate([-sin, sin], axis=1), (1, LANES // HEAD_DIM))
    return cos_t, sin_t


def _routing_tables(ri, cnt, n_tiles_max):
    counts = cnt[:, 0]
    padded = (counts + TM_EXP - 1) // TM_EXP * TM_EXP
    seg_end = jnp.cumsum(padded)
    seg_start = seg_end - padded
    pos1 = seg_start[ri[0]] + ri[2]
    pos2 = seg_start[ri[1]] + ri[3]
    n_active = seg_end[-1] // TM_EXP
    tile_row = jnp.minimum(jnp.arange(n_tiles_max, dtype=jnp.int32), n_active - 1) * TM_EXP
    tile_expert = jnp.sum((tile_row[:, None] >= seg_end[None, :]).astype(jnp.int32), axis=1)
    first = jnp.any(tile_row[:, None] == seg_start[None, :], axis=1) & (padded[tile_expert] > 0)
    meta = jnp.concatenate([n_active[None].astype(jnp.int32), tile_expert.astype(jnp.int32),
                            first.astype(jnp.int32)])
    return jnp.concatenate([pos1, pos2]).astype(jnp.int32), meta


def kernel(x_prompt, x_sample, c_prompt, c_sample, w_ada, b_ada, norm1_g, w_in, lambda_q1, lambda_k1,
           lambda_q2, lambda_k2, subln_g, gmlp_norm_g, w_spatial, b_spatial, w_branch_att,
           w_branch_gmlp, w_out, norm2_g, w_router_group, w_router_expert, w_gate, w_up, w_down,
           final_norm_g):
    bf = jnp.bfloat16
    bp, sp, d = x_prompt.shape
    bs, ss, _ = x_sample.shape
    assert bp == 1 and d == D_MODEL
    rows = _Rows(bp * sp, bs, ss)
    xp = x_prompt.reshape(bp * sp, d)
    xs = x_sample.reshape(bs * ss, d)

    c8 = jnp.concatenate([c_prompt, c_sample, jnp.zeros((8 - bp - bs, d), jnp.float32)], axis=0)
    mod3 = _ada(c8, w_ada[0], b_ada).reshape(8, 6, d)

    cos_t, sin_t = _rope_tables(max(sp, ss))
    z = _inproj(rows, xp, xs, mod3, norm1_g, w_in[0].astype(bf), cos_t, sin_t, gmlp_norm_g)

    lam_p = jnp.concatenate([lambda_q1, lambda_k1, lambda_q2, lambda_k2], axis=0)
    a_p = _attn(z, lam_p, subln_g, bp, sp, 0)
    a_s = _attn(z, lam_p, subln_g, bs, ss, bp * sp)

    bias_tile = jnp.repeat(b_spatial[0].T, GMLP_GROUP_DIM, axis=1)
    merged = _mix(rows, a_p, a_s, z, w_spatial[0].astype(bf), bias_tile,
                  w_branch_att[0].astype(bf), w_branch_gmlp[0].astype(bf))

    wr = jnp.concatenate([w_router_expert[0], w_router_group[0],
                          jnp.zeros((d, LANES - N_EXPERTS - N_GROUPS), jnp.float32)], axis=1)
    wr_hi = wr.astype(bf)
    wr_lo = (wr - wr_hi.astype(jnp.float32)).astype(bf)
    wr_cat = jnp.concatenate([wr_hi, wr_lo], axis=1)
    tri = jnp.triu(jnp.ones((TM_OUT, TM_OUT), bf), k=1)
    x1, h2, ri, rf, cnt = _outproj(rows, merged, xp, xs, mod3, norm2_g, w_out[0].astype(bf),
                                   wr_cat, tri)

    n_rows = (2 * rows.t // TM_EXP + N_EXPERTS) * TM_EXP
    pos_flat, meta = _routing_tables(ri, cnt, n_rows // TM_EXP)
    xs_sorted = _dispatch(pos_flat, h2, n_rows)
    ys = _experts(meta, xs_sorted, w_gate[0], w_up[0], w_down[0])
    y_p, y_s = _combine(rows, pos_flat, ys, x1, rf, mod3, final_norm_g.reshape(1, d))
    return y_p.reshape(bp, sp, d), y_s.reshape(bs, ss, d)
```

```python
import functools
import math

import jax
import jax.numpy as jnp
from jax import lax
from jax.experimental import pallas as pl
from jax.experimental.pallas import tpu as pltpu

D_MODEL = 2048
N_HEADS = 8
HEAD_DIM = 64
V_DIM = 2 * HEAD_DIM
QK_WIDTH = N_HEADS * 2 * HEAD_DIM
V_WIDTH = N_HEADS * V_DIM
ROPE_THETA = 10000.0
GMLP_GROUPS = 8
GMLP_GROUP_DIM = 128
GMLP_WIDTH = GMLP_GROUPS * GMLP_GROUP_DIM
GMLP_CHUNK = 128
IN_COLS = 3 * 1024 + 2 * 1024 + 2 * D_MODEL
N_GROUPS = 4
EXPERTS_PER_GROUP = 8
N_EXPERTS = N_GROUPS * EXPERTS_PER_GROUP
D_EXPERT = 512
EPS = 1e-6
LAMBDA_INIT = 0.8 - 0.6 * math.exp(-0.3 * 0)

LANES = 128
COL_TILE = 1024
N_COL_TILES = IN_COLS // COL_TILE
VMEM_LIMIT = 56 * 1024 * 1024

TM_IN = 1024
TM_MIX = 512
TM_OUT = 256
TQ = 512
TK = 1024
TM_EXP = 256
TM_DISP = 512
TM_COMB = 256


def _cparams(sem, flags=None):
    return pltpu.CompilerParams(dimension_semantics=sem, vmem_limit_bytes=VMEM_LIMIT, flags=flags)


def _rms(x, g):
    return x * lax.rsqrt(jnp.mean(x * x, axis=-1, keepdims=True) + EPS) * g


def _pack_rows(x):
    n = x.shape[1] // 2
    hi = lax.bitcast_convert_type(x[:, :n].astype(jnp.bfloat16).astype(jnp.float32), jnp.uint32)
    lo = lax.bitcast_convert_type(x[:, n:].astype(jnp.bfloat16).astype(jnp.float32), jnp.uint32)
    return hi | (lo >> 16)


def _unpack_rows(w):
    a = lax.bitcast_convert_type(w & jnp.uint32(0xFFFF0000), jnp.float32)
    b = lax.bitcast_convert_type(w << 16, jnp.float32)
    return a, b


def _ada_kernel(c_ref, w_ref, b_ref, o_ref):
    c = c_ref[...]
    s = (c / (1.0 + jnp.exp(-c))).astype(jnp.bfloat16)
    o_ref[...] = jnp.dot(s, w_ref[...].astype(jnp.bfloat16),
                         preferred_element_type=jnp.float32) + b_ref[...]


def _ada(c8, w_ada, b_ada):
    n = w_ada.shape[1]
    tn = 1024
    return pl.pallas_call(
        _ada_kernel,
        grid=(n // tn,),
        in_specs=[pl.BlockSpec((8, D_MODEL), lambda j: (0, 0)),
                  pl.BlockSpec((D_MODEL, tn), lambda j: (0, j)),
                  pl.BlockSpec((1, tn), lambda j: (0, j))],
        out_specs=pl.BlockSpec((8, tn), lambda j: (0, j)),
        out_shape=jax.ShapeDtypeStruct((8, n), jnp.float32),
        compiler_params=_cparams(("arbitrary",)),
        name="ada",
    )(c8, w_ada, b_ada)


class _Rows:
    def __init__(self, n_prompt, n_sample_seq, sample_len):
        self.np = n_prompt
        self.ns = n_sample_seq
        self.ls = sample_len
        self.t = n_prompt + n_sample_seq * sample_len

    def seq_of_tile(self, i, tm):
        npt = self.np // tm
        return jnp.where(i < npt, 0, 1 + (i - npt) // (self.ls // tm))

    def pos_tile(self, i, tm):
        npt = self.np // tm
        return jnp.where(i < npt, i, (i - npt) % (self.ls // tm))

    def prompt_tile(self, i, tm):
        return jnp.minimum(i, self.np // tm - 1)

    def sample_tile(self, i, tm):
        return jnp.maximum(i - self.np // tm, 0)


def _inproj_kernel(rows, xp_ref, xs_ref, mod_ref, g1_ref, w_ref, cos_ref, sin_ref, gn_ref,
                   z_ref, h_ref):
    i = pl.program_id(0)
    j = pl.program_id(1)
    tm = h_ref.shape[0]

    @pl.when(j == 0)
    def _():
        m = mod_ref[0]
        x = jnp.where(i < rows.np // tm, xp_ref[...], xs_ref[...])
        h = _rms(x, g1_ref[...]) * (1.0 + m[1:2, :]) + m[0:1, :]
        h_ref[...] = h.astype(jnp.bfloat16)

    def zdot():
        return jnp.dot(h_ref[...], w_ref[...], preferred_element_type=jnp.float32)

    def rope(scale):
        z = zdot()
        c = cos_ref[...]
        s = sin_ref[...]
        lane = lax.broadcasted_iota(jnp.int32, (tm, LANES), 1)
        first_half = (lane % HEAD_DIM) < (HEAD_DIM // 2)
        for b in range(COL_TILE // LANES):
            zb = z[:, b * LANES:(b + 1) * LANES]
            up = pltpu.roll(zb, LANES - HEAD_DIM // 2, 1)
            dn = pltpu.roll(zb, HEAD_DIM // 2, 1)
            r = (zb * c + jnp.where(first_half, up, dn) * s) * scale
            z_ref[:, b * LANES:(b + 1) * LANES] = r.astype(jnp.bfloat16)

    @pl.when(j == 0)
    def _():
        rope(HEAD_DIM ** -0.5 * math.log2(math.e))

    @pl.when(j == 1)
    def _():
        rope(1.0)

    @pl.when(j == 2)
    def _():
        z_ref[...] = zdot().astype(jnp.bfloat16)

    @pl.when(j == 3)
    def _():
        z_ref[...] = jax.nn.gelu(zdot(), approximate=True).astype(jnp.bfloat16)

    @pl.when(j == 4)
    def _():
        z_ref[...] = _rms(jax.nn.gelu(zdot(), approximate=True), gn_ref[...]).astype(jnp.bfloat16)

    @pl.when(j >= 5)
    def _():
        z_ref[...] = (1.0 / (1.0 + jnp.exp(-zdot()))).astype(jnp.bfloat16)


def _inproj(rows, xp, xs, mod3, g1, w_in_bf, cos_t, sin_t, gn):
    tm = TM_IN
    return pl.pallas_call(
        functools.partial(_inproj_kernel, rows),
        grid=(rows.t // tm, N_COL_TILES),
        in_specs=[
            pl.BlockSpec((tm, D_MODEL), lambda i, j: (rows.prompt_tile(i, tm), 0),
                         pipeline_mode=pl.Buffered(1)),
            pl.BlockSpec((tm, D_MODEL), lambda i, j: (rows.sample_tile(i, tm), 0),
                         pipeline_mode=pl.Buffered(1)),
            pl.BlockSpec((1, 6, D_MODEL), lambda i, j: (rows.seq_of_tile(i, tm), 0, 0)),
            pl.BlockSpec((1, D_MODEL), lambda i, j: (0, 0)),
            pl.BlockSpec((D_MODEL, COL_TILE), lambda i, j: (0, j)),
            pl.BlockSpec((tm, LANES), lambda i, j: (rows.pos_tile(i, tm), 0)),
            pl.BlockSpec((tm, LANES), lambda i, j: (rows.pos_tile(i, tm), 0)),
            pl.BlockSpec((1, COL_TILE), lambda i, j: (0, 0)),
        ],
        out_specs=pl.BlockSpec((tm, COL_TILE), lambda i, j: (i, j)),
        out_shape=jax.ShapeDtypeStruct((rows.t, IN_COLS), jnp.bfloat16),
        scratch_shapes=[pltpu.VMEM((tm, D_MODEL), jnp.bfloat16)],
        compiler_params=_cparams(("arbitrary", "arbitrary")),
        name="inproj",
    )(xp, xs, mod3, g1, w_in_bf, cos_t, sin_t, gn)


ONES_ROWS = 16


def _attn_kernel(n_k, q_ref, k_ref, v_ref, lam_ref, sg_ref, o_ref, qq_ref, vt_ref, acc_ref, m_ref,
                 sa_ref, sb_ref):
    tq = q_ref.shape[0]
    tk = sa_ref.shape[0]

    @pl.when(pl.program_id(2) == 0)
    def _():
        vt_ref[V_DIM:, :] = jnp.ones((ONES_ROWS, vt_ref.shape[1]), jnp.bfloat16)

        def transpose_chunk(c, carry):
            off = pl.multiple_of(c * tk, tk)
            vc = v_ref[pl.ds(off, tk), :].astype(jnp.float32)
            vt_ref[0:V_DIM, pl.ds(off, tk)] = vc.T.astype(jnp.bfloat16)
            return carry

        lax.fori_loop(0, n_k, transpose_chunk, 0)

    q = q_ref[...]
    lane = lax.broadcasted_iota(jnp.int32, q.shape, 1)
    zero = jnp.zeros_like(q)
    qq_ref[0:tq, :] = jnp.where(lane < HEAD_DIM, q, zero)
    qq_ref[tq:2 * tq, :] = jnp.where(lane >= HEAD_DIM, q, zero)
    m_ref[...] = jnp.full(m_ref.shape, -jnp.inf, jnp.float32)
    acc_ref[...] = jnp.zeros(acc_ref.shape, jnp.float32)

    def scores(ik, s_ref):
        off = pl.multiple_of(ik * tk, tk)
        s_ref[...] = lax.dot_general(k_ref[pl.ds(off, tk), :], qq_ref[...], (((1,), (1,)), ((), ())),
                                     preferred_element_type=jnp.float32)

    def accumulate(ik, s_ref):
        off = pl.multiple_of(ik * tk, tk)
        s = s_ref[...]
        m_old = m_ref[...]
        m_new = jnp.maximum(m_old, jnp.max(s, axis=0, keepdims=True))
        p = jnp.exp2(s - m_new).astype(jnp.bfloat16)
        alpha = jnp.exp2(m_old - m_new)
        m_ref[...] = m_new
        pv = jnp.dot(vt_ref[:, pl.ds(off, tk)], p, preferred_element_type=jnp.float32)
        acc_ref[...] = alpha * acc_ref[...] + pv

    def chunk_pair(ik, last):
        scores(ik + 1, sb_ref)
        accumulate(ik, sa_ref)
        if not last:
            scores(ik + 2, sa_ref)
        accumulate(ik + 1, sb_ref)

    scores(0, sa_ref)

    def body(i, carry):
        chunk_pair(2 * i, False)
        return carry

    lax.fori_loop(0, n_k // 2 - 1, body, 0)
    chunk_pair(n_k - 2, True)

    lp = lam_ref[...]
    lam = (jnp.exp(jnp.sum(lp[0:1, :] * lp[1:2, :], axis=1, keepdims=True))
           - jnp.exp(jnp.sum(lp[2:3, :] * lp[3:4, :], axis=1, keepdims=True)) + LAMBDA_INIT)
    acc = acc_ref[0:V_DIM, :]
    l = acc_ref[V_DIM:V_DIM + 1, :]
    o_t =acc[:, 0:tq] / l[:, 0:tq] - lam * (acc[:, tq:2 * tq] / l[:, tq:2 * tq])
    o = o_t.T
    o_ref[...] = (_rms(o, sg_ref[...]) * (1.0 - LAMBDA_INIT)).astype(jnp.bfloat16)


def _attn(z, lam_p, subln_g, n_seq, seq_len, row0):
    tq = min(TQ, seq_len)
    nq = seq_len // tq
    tk = min(TK, seq_len // 2)
    assert seq_len % (2 * tk) == 0 and row0 % seq_len == 0
    qb0 = row0 // tq
    sb0 = row0 // seq_len
    kern = functools.partial(_attn_kernel, seq_len // tk)
    return pl.pallas_call(
        kern,
        grid=(n_seq, N_HEADS, nq),
        in_specs=[
            pl.BlockSpec((tq, LANES), lambda b, h, iq: (qb0 + b * nq + iq, h)),
            pl.BlockSpec((seq_len, LANES), lambda b, h, iq: (sb0 + b, N_HEADS + h)),
            pl.BlockSpec((seq_len, LANES), lambda b, h, iq: (sb0 + b, 2 * N_HEADS + h)),
            pl.BlockSpec((4, HEAD_DIM), lambda b, h, iq: (0, 0)),
            pl.BlockSpec((1, V_DIM), lambda b, h, iq: (0, 0)),
        ],
        out_specs=pl.BlockSpec((tq, LANES), lambda b, h, iq: (b * nq + iq, h)),
        out_shape=jax.ShapeDtypeStruct((n_seq * seq_len, V_WIDTH), jnp.bfloat16),
        scratch_shapes=[pltpu.VMEM((2 * tq, LANES), jnp.bfloat16),
                        pltpu.VMEM((V_DIM + ONES_ROWS, seq_len), jnp.bfloat16),
                        pltpu.VMEM((V_DIM + ONES_ROWS, 2 * tq), jnp.float32),
                        pltpu.VMEM((1, 2 * tq), jnp.float32),
                        pltpu.VMEM((tk, 2 * tq), jnp.float32),
                        pltpu.VMEM((tk, 2 * tq), jnp.float32)],
        compiler_params=_cparams(("arbitrary", "arbitrary", "arbitrary")),
        name="attn",
    )(z, z, z, lam_p, subln_g)


def _mix_kernel(ap_ref, as_ref, gu_ref, gv_ref, sa0_ref, sa1_ref, sb0_ref, sb1_ref,
                ws_ref, bs_ref, wba_ref, wbg_ref, o_ref, m_ref, *, n_prompt_tiles):
    i = pl.program_id(0)
    tm = gu_ref.shape[0]
    bias = bs_ref[...]
    for c in range(tm // GMLP_CHUNK):
        r0 = c * GMLP_CHUNK
        for g in range(GMLP_GROUPS):
            c0 = g * GMLP_GROUP_DIM
            mixed = jnp.dot(ws_ref[g], gv_ref[r0:r0 + GMLP_CHUNK, c0:c0 + GMLP_GROUP_DIM],
                            preferred_element_type=jnp.float32) + bias[:, c0:c0 + GMLP_GROUP_DIM]
            gu = gu_ref[r0:r0 + GMLP_CHUNK, c0:c0 + GMLP_GROUP_DIM].astype(jnp.float32)
            m_ref[r0:r0 + GMLP_CHUNK, c0:c0 + GMLP_GROUP_DIM] = (gu * mixed).astype(jnp.bfloat16)

    a = jnp.where(i < n_prompt_tiles, ap_ref[...], as_ref[...])
    m = m_ref[...]
    for nb, (sa_ref, sb_ref) in enumerate(((sa0_ref, sb0_ref), (sa1_ref, sb1_ref))):
        cs = slice(nb * COL_TILE, (nb + 1) * COL_TILE)
        pa = jnp.dot(a, wba_ref[:, cs], preferred_element_type=jnp.float32)
        pb = jnp.dot(m, wbg_ref[:, cs], preferred_element_type=jnp.float32)
        merged = sa_ref[...].astype(jnp.float32) * pa + sb_ref[...].astype(jnp.float32) * pb
        o_ref[:, cs] = merged.astype(jnp.bfloat16)


def _mix(rows, a_p, a_s, z, ws_bf, bias_tile, wba_bf, wbg_bf):
    tm = TM_MIX
    zcol = lambda cb: pl.BlockSpec((tm, COL_TILE), lambda i: (i, cb))
    return pl.pallas_call(
        functools.partial(_mix_kernel, n_prompt_tiles=rows.np // tm),
        grid=(rows.t // tm,),
        in_specs=[
            pl.BlockSpec((tm, V_WIDTH), lambda i: (rows.prompt_tile(i, tm), 0)),
            pl.BlockSpec((tm, V_WIDTH), lambda i: (rows.sample_tile(i, tm), 0)),
            zcol(3), zcol(4), zcol(5), zcol(6), zcol(7), zcol(8),
            pl.BlockSpec((GMLP_GROUPS, GMLP_CHUNK, GMLP_CHUNK), lambda i: (0, 0, 0)),
            pl.BlockSpec((GMLP_CHUNK, GMLP_WIDTH), lambda i: (0, 0)),
            pl.BlockSpec((V_WIDTH, D_MODEL), lambda i: (0, 0)),
            pl.BlockSpec((GMLP_WIDTH, D_MODEL), lambda i: (0, 0)),
        ],
        out_specs=pl.BlockSpec((tm, D_MODEL), lambda i: (i, 0)),
        out_shape=jax.ShapeDtypeStruct((rows.t, D_MODEL), jnp.bfloat16),
        scratch_shapes=[pltpu.VMEM((tm, GMLP_WIDTH), jnp.bfloat16)],
        compiler_params=_cparams(("arbitrary",)),
        name="mix",
    )(a_p, a_s, z, z, z, z, z, z, ws_bf, bias_tile, wba_bf, wbg_bf)


def _outproj_kernel(rows, mg_ref, xp_ref, xs_ref, mod_ref, g2_ref, wo_ref, wr_ref, tri_ref,
                    x1_ref, h2_ref, ri_ref, rf_ref, cnt_ref, base_ref):
    i = pl.program_id(0)
    tm = mg_ref.shape[0]

    @pl.when(i == 0)
    def _():
        base_ref[...] = jnp.zeros(base_ref.shape, jnp.float32)

    m = mod_ref[0]
    y = jnp.dot(mg_ref[...], wo_ref[...], preferred_element_type=jnp.float32)

    x = jnp.where(i < rows.np // tm, xp_ref[...], xs_ref[...])
    x1 = x + m[2:3, :] * y
    x1_ref[...] = x1
    h2 = _rms(x1, g2_ref[...]) * (1.0 + m[4:5, :]) + m[3:4, :]
    h2_ref[...] = _pack_rows(h2)

    h_hi = h2.astype(jnp.bfloat16)
    h_lo = (h2 - h_hi.astype(jnp.float32)).astype(jnp.bfloat16)
    hh = jnp.dot(h_hi, wr_ref[...], preferred_element_type=jnp.float32)
    lh = jnp.dot(h_lo, wr_ref[:, 0:LANES], preferred_element_type=jnp.float32)
    lg = (hh[:, 0:LANES] + hh[:, LANES:2 * LANES] + lh).T

    row8 = lax.broadcasted_iota(jnp.int32, (8, tm), 0)
    neg = jnp.float32(-jnp.inf)
    gl = jnp.where(row8 < N_GROUPS, lg[N_EXPERTS:N_EXPERTS + 8, :], neg)
    gmax = jnp.max(gl, axis=0, keepdims=True)
    g_top = 1.0 / jnp.sum(jnp.exp(gl - gmax), axis=0, keepdims=True)
    g_idx = jnp.min(jnp.where(gl == gmax, row8, 8), axis=0, keepdims=True)
    e_sel = jnp.zeros((8, tm), jnp.float32)
    for g in range(N_GROUPS):
        e_sel = jnp.where(g_idx == g, lg[g * EXPERTS_PER_GROUP:(g + 1) * EXPERTS_PER_GROUP, :], e_sel)
    e1 = jnp.max(e_sel, axis=0, keepdims=True)
    i1 = jnp.min(jnp.where(e_sel == e1, row8, 8), axis=0, keepdims=True)
    e_rest = jnp.where(row8 == i1, neg, e_sel)
    e2 = jnp.max(e_rest, axis=0, keepdims=True)
    i2 = jnp.min(jnp.where(e_rest == e2, row8, 8), axis=0, keepdims=True)
    t = jnp.exp(e2 - e1)
    w1 = g_top / (1.0 + t)
    w2 = g_top * t / (1.0 + t)
    id1 = g_idx * EXPERTS_PER_GROUP + i1
    id2 = g_idx * EXPERTS_PER_GROUP + i2

    row32 = lax.broadcasted_iota(jnp.int32, (N_EXPERTS, tm), 0)
    oh1 = (row32 == id1).astype(jnp.float32)
    oh2 = (row32 == id2).astype(jnp.float32)
    both = oh1 + oh2
    before = jnp.dot(both.astype(jnp.bfloat16), tri_ref[...], preferred_element_type=jnp.float32)
    base = base_ref[...]
    tot = before + jnp.concatenate([base] * (tm // LANES), axis=1)
    rank1 = jnp.sum(oh1 * tot, axis=0, keepdims=True)
    rank2 = jnp.sum(oh2 * tot, axis=0, keepdims=True)
    new_base = base + jnp.sum(both, axis=1, keepdims=True)
    base_ref[...] = new_base
    cnt_ref[...] = new_base.astype(jnp.int32)

    zi = jnp.zeros((4, tm), jnp.int32)
    ri_ref[...] = jnp.concatenate([id1, id2, rank1.astype(jnp.int32), rank2.astype(jnp.int32), zi], axis=0)
    rf_ref[...] = jnp.concatenate([w1, w2, jnp.zeros((6, tm), jnp.float32)], axis=0)


def _outproj(rows, merged, xp, xs, mod3, g2, wo_bf, wr_cat, tri):
    tm = TM_OUT
    t = rows.t
    return pl.pallas_call(
        functools.partial(_outproj_kernel, rows),
        grid=(t // tm,),
        in_specs=[
            pl.BlockSpec((tm, D_MODEL), lambda i: (i, 0)),
            pl.BlockSpec((tm, D_MODEL), lambda i: (rows.prompt_tile(i, tm), 0)),
            pl.BlockSpec((tm, D_MODEL), lambda i: (rows.sample_tile(i, tm), 0)),
            pl.BlockSpec((1, 6, D_MODEL), lambda i: (rows.seq_of_tile(i, tm), 0, 0)),
            pl.BlockSpec((1, D_MODEL), lambda i: (0, 0)),
            pl.BlockSpec((D_MODEL, D_MODEL), lambda i: (0, 0)),
            pl.BlockSpec((D_MODEL, 2 * LANES), lambda i: (0, 0)),
            pl.BlockSpec((tm, tm), lambda i: (0, 0)),
        ],
        out_specs=[
            pl.BlockSpec((tm, D_MODEL), lambda i: (i, 0)),
            pl.BlockSpec((tm, D_MODEL // 2), lambda i: (i, 0)),
            pl.BlockSpec((8, tm), lambda i: (0, i)),
            pl.BlockSpec((8, tm), lambda i: (0, i)),
            pl.BlockSpec((N_EXPERTS, LANES), lambda i: (0, 0)),
        ],
        out_shape=[
            jax.ShapeDtypeStruct((t, D_MODEL), jnp.float32),
            jax.ShapeDtypeStruct((t, D_MODEL // 2), jnp.uint32),
            jax.ShapeDtypeStruct((8, t), jnp.int32),
            jax.ShapeDtypeStruct((8, t), jnp.float32),
            jax.ShapeDtypeStruct((N_EXPERTS, LANES), jnp.int32),
        ],
        scratch_shapes=[pltpu.VMEM((N_EXPERTS, LANES), jnp.float32)],
        compiler_params=_cparams(("arbitrary",)),
        name="outproj",
    )(merged, xp, xs, mod3, g2, wo_bf, wr_cat, tri)


def _dispatch_kernel(pos_ref, seg_ref, h2_ref, xs_ref, zero_ref, sem, zsem):
    i = pl.program_id(0)
    t_total = pl.num_programs(0) * TM_DISP
    t0 = i * TM_DISP
    n_tiles = xs_ref.shape[0] // TM_EXP

    @pl.when(i == 0)
    def _():
        zero_ref[...] = jnp.zeros(zero_ref.shape, zero_ref.dtype)

        def zero_tile(row):
            return pltpu.make_async_copy(zero_ref, xs_ref.at[pl.ds(row, TM_EXP), :], zsem)

        def for_each_zeroed_tile(action):
            for e in range(N_EXPERTS):
                @pl.when(seg_ref[1 + N_EXPERTS + e] == 1)
                def _():
                    action(pl.multiple_of(seg_ref[1 + e], TM_EXP))

            def tail(j, carry):
                @pl.when(j >= seg_ref[0])
                def _():
                    action(pl.multiple_of(j * TM_EXP, TM_EXP))
                return carry

            lax.fori_loop(0, n_tiles, tail, 0)

        for_each_zeroed_tile(lambda row: zero_tile(row).start())
        for_each_zeroed_tile(lambda row: zero_tile(row).wait())

    def row_copy(r, dst):
        return pltpu.make_async_copy(h2_ref.at[pl.ds(r, 1), :], xs_ref.at[pl.ds(dst, 1), :], sem)

    for r in range(TM_DISP):
        row_copy(r, pos_ref[t0 + r]).start(priority=0)
        row_copy(r, pos_ref[t_total + t0 + r]).start(priority=1)

    for _ in range(2 * TM_DISP):
        row_copy(0, 0).wait()


def _dispatch(pos_flat, seg, h2, n_rows):
    t, width = h2.shape
    return pl.pallas_call(
        _dispatch_kernel,
        grid_spec=pltpu.PrefetchScalarGridSpec(
            num_scalar_prefetch=2,
            grid=(t // TM_DISP,),
            in_specs=[pl.BlockSpec((TM_DISP, width), lambda i, pos, seg: (i, 0))],
            out_specs=pl.BlockSpec(memory_space=pl.ANY),
            scratch_shapes=[pltpu.VMEM((TM_EXP, width), h2.dtype),
                            pltpu.SemaphoreType.DMA, pltpu.SemaphoreType.DMA],
        ),
        out_shape=jax.ShapeDtypeStruct((n_rows, width), h2.dtype),
        compiler_params=_cparams(("arbitrary",)),
        name="dispatch",
    )(pos_flat, seg, h2)


def _experts_kernel(meta_ref, x_ref, wg_ref, wu_ref, wd_ref, y_ref, wg_s, wu_s, wd_s):
    j = pl.program_id(0)
    n_tiles = pl.num_programs(0)
    active = j < meta_ref[0]

    @pl.when(active & (meta_ref[1 + n_tiles + j] == 1))
    def _():
        wg_s[...] = wg_ref[0].astype(jnp.bfloat16)
        wu_s[...] = wu_ref[0].astype(jnp.bfloat16)
        wd_s[...] = wd_ref[0].astype(jnp.bfloat16)

    @pl.when(active)
    def _():
        xa, xb = _unpack_rows(x_ref[...])
        x = jnp.concatenate([xa.astype(jnp.bfloat16), xb.astype(jnp.bfloat16)], axis=1)
        g = jnp.dot(x, wg_s[...], preferred_element_type=jnp.float32)
        u = jnp.dot(x, wu_s[...], preferred_element_type=jnp.float32)
        hm = (g / (1.0 + jnp.exp(-g)) * u).astype(jnp.bfloat16)
        y_ref[...] = _pack_rows(jnp.dot(hm, wd_s[...], preferred_element_type=jnp.float32))

    @pl.when(jnp.logical_not(active))
    def _():
        y_ref[...] = jnp.zeros(y_ref.shape, jnp.uint32)


def _experts(meta, xs, w_gate, w_up, w_down):
    n_rows = xs.shape[0]
    n_tiles = n_rows // TM_EXP
    row = lambda j, meta: (jnp.minimum(j, meta[0] - 1), 0)
    wsel = lambda j, meta: (meta[1 + j], 0, 0)
    return pl.pallas_call(
        _experts_kernel,
        grid_spec=pltpu.PrefetchScalarGridSpec(
            num_scalar_prefetch=1,
            grid=(n_tiles,),
            in_specs=[pl.BlockSpec((TM_EXP, D_MODEL // 2), row),
                      pl.BlockSpec((1, D_MODEL, D_EXPERT), wsel),
                      pl.BlockSpec((1, D_MODEL, D_EXPERT), wsel),
                      pl.BlockSpec((1, D_EXPERT, D_MODEL), wsel)],
            out_specs=pl.BlockSpec((TM_EXP, D_MODEL // 2), lambda j, meta: (j, 0)),
            scratch_shapes=[pltpu.VMEM((D_MODEL, D_EXPERT), jnp.bfloat16),
                            pltpu.VMEM((D_MODEL, D_EXPERT), jnp.bfloat16),
                            pltpu.VMEM((D_EXPERT, D_MODEL), jnp.bfloat16)],
        ),
        out_shape=jax.ShapeDtypeStruct((n_rows, D_MODEL // 2), jnp.uint32),
        compiler_params=_cparams(("arbitrary",)),
        name="experts",
    )(meta, xs, w_gate, w_up, w_down)


COMB_SETS = 3
COMB_CHUNK = 32


def _combine_kernel(rows, pos_ref, ys_ref, x1_ref, rf_ref, mod_ref, gf_ref, op_ref, os_ref,
                    buf_ref, out_ref, sems):
    i = pl.program_id(0)
    n = pl.num_programs(0)
    tm = x1_ref.shape[0]

    def row_copy(src, st, slot, r):
        return pltpu.make_async_copy(ys_ref.at[pl.ds(src, 1), :],
                                     buf_ref.at[2 * st + slot, pl.ds(r, 1), :], sems.at[st])

    def fetch_row(tile, st, r):
        t0 = jnp.minimum(tile, n - 1) * tm
        row_copy(pos_ref[t0 + r], st, 0, r).start()
        row_copy(pos_ref[rows.t + t0 + r], st, 1, r).start()

    def drain(st):
        for _ in range(2 * tm):
            row_copy(0, st, 0, 0).wait()

    @pl.when(i == 0)
    def _():
        def first_two(r, carry):
            fetch_row(0, 0, r)
            fetch_row(1, 1, r)
            return carry

        lax.fori_loop(0, tm, first_two, 0, unroll=8)

    cur = lax.rem(i, COMB_SETS)
    nxt = lax.rem(i + 2, COMB_SETS)
    drain(cur)

    w = rf_ref[...].T
    g2 = mod_ref[0][5:6, :]
    gf = gf_ref[...]
    for c in range(tm // COMB_CHUNK):
        rs = slice(c * COMB_CHUNK, (c + 1) * COMB_CHUNK)
        y1a, y1b = _unpack_rows(buf_ref[2 * cur, rs, :])
        y2a, y2b = _unpack_rows(buf_ref[2 * cur + 1, rs, :])
        w1 = w[rs, 0:1]
        w2 = w[rs, 1:2]
        moe = jnp.concatenate([w1 * y1a + w2 * y2a, w1 * y1b + w2 * y2b], axis=1)
        out_ref[rs, :] = _rms(x1_ref[rs, :] + g2 * moe, gf)
        for r in range(c * COMB_CHUNK, (c + 1) * COMB_CHUNK):
            fetch_row(i + 2, nxt, r)

    @pl.when(i < rows.np // tm)
    def _():
        op_ref[...] = out_ref[...]

    @pl.when(i >= rows.np // tm)
    def _():
        os_ref[...] = out_ref[...]

    @pl.when(i == n - 1)
    def _():
        drain(lax.rem(i + 1, COMB_SETS))
        drain(nxt)


def _combine(rows, pos_flat, ys, x1, rf, mod3, gf):
    tm = TM_COMB
    return pl.pallas_call(
        functools.partial(_combine_kernel, rows),
        grid_spec=pltpu.PrefetchScalarGridSpec(
            num_scalar_prefetch=1,
            grid=(rows.t // tm,),
            in_specs=[
                pl.BlockSpec(memory_space=pl.ANY),
                pl.BlockSpec((tm, D_MODEL), lambda i, pos: (i, 0)),
                pl.BlockSpec((8, tm), lambda i, pos: (0, i)),
                pl.BlockSpec((1, 6, D_MODEL), lambda i, pos: (rows.seq_of_tile(i, tm), 0, 0)),
                pl.BlockSpec((1, D_MODEL), lambda i, pos: (0, 0)),
            ],
            out_specs=[
                pl.BlockSpec((tm, D_MODEL), lambda i, pos: (rows.prompt_tile(i, tm), 0)),
                pl.BlockSpec((tm, D_MODEL), lambda i, pos: (rows.sample_tile(i, tm), 0)),
            ],
            scratch_shapes=[pltpu.VMEM((2 * COMB_SETS, tm, D_MODEL // 2), jnp.uint32),
                            pltpu.VMEM((tm, D_MODEL), jnp.float32),
                            pltpu.SemaphoreType.DMA((COMB_SETS,))],
        ),
        out_shape=[jax.ShapeDtypeStruct((rows.np, D_MODEL), jnp.float32),
                   jax.ShapeDtypeStruct((rows.t - rows.np, D_MODEL), jnp.float32)],
        compiler_params=_cparams(("arbitrary",)),
        name="combine",
    )(pos_flat, ys, x1, rf, mod3, gf)


def _rope_tables(seq_len):
    pos = jnp.arange(seq_len, dtype=jnp.float32)
    inv = 1.0 / (ROPE_THETA ** (jnp.arange(0, HEAD_DIM, 2, dtype=jnp.float32) / HEAD_DIM))
    ang = pos[:, None] * inv[None, :]
    cos, sin = jnp.cos(ang), jnp.sin(ang)
    cos_t = jnp.tile(cos, (1, LANES // (HEAD_DIM // 2)))
    sin_t = jnp.tile(jnp.concatenate([-sin, sin], axis=1), (1, LANES // HEAD_DIM))
    return cos_t, sin_t


def _routing_tables(ri, cnt, n_tiles_max):
    counts = cnt[:, 0]
    padded = (counts + TM_EXP - 1) // TM_EXP * TM_EXP
    seg_end = jnp.cumsum(padded)
    seg_start = seg_end - padded
    pos1 = seg_start[ri[0]] + ri[2]
    pos2 = seg_start[ri[1]] + ri[3]
    n_active = seg_end[-1] // TM_EXP
    tile_row = jnp.minimum(jnp.arange(n_tiles_max, dtype=jnp.int32), n_active - 1) * TM_EXP
    tile_expert = jnp.sum((tile_row[:, None] >= seg_end[None, :]).astype(jnp.int32), axis=1)
    first = jnp.any(tile_row[:, None] == seg_start[None, :], axis=1) & (padded[tile_expert] > 0)
    meta = jnp.concatenate([n_active[None].astype(jnp.int32), tile_expert.astype(jnp.int32),
                            first.astype(jnp.int32)])
    seg = jnp.concatenate([n_active[None], jnp.maximum(seg_end - TM_EXP, 0),
                           (padded > 0).astype(jnp.int32)]).astype(jnp.int32)
    return jnp.concatenate([pos1, pos2]).astype(jnp.int32), meta, seg


def kernel(x_prompt, x_sample, c_prompt, c_sample, w_ada, b_ada, norm1_g, w_in, lambda_q1, lambda_k1,
           lambda_q2, lambda_k2, subln_g, gmlp_norm_g, w_spatial, b_spatial, w_branch_att,
           w_branch_gmlp, w_out, norm2_g, w_router_group, w_router_expert, w_gate, w_up, w_down,
           final_norm_g):
    bf = jnp.bfloat16
    bp, sp, d = x_prompt.shape
    bs, ss, _ = x_sample.shape
    assert bp == 1 and d == D_MODEL
    rows = _Rows(bp * sp, bs, ss)
    xp = x_prompt.reshape(bp * sp, d)
    xs = x_sample.reshape(bs * ss, d)

    c8 = jnp.concatenate([c_prompt, c_sample, jnp.zeros((8 - bp - bs, d), jnp.float32)], axis=0)
    mod3 = _ada(c8, w_ada[0], b_ada).reshape(8, 6, d)

    cos_t, sin_t = _rope_tables(max(sp, ss))
    z = _inproj(rows, xp, xs, mod3, norm1_g, w_in[0].astype(bf), cos_t, sin_t, gmlp_norm_g)

    lam_p = jnp.concatenate([lambda_q1, lambda_k1, lambda_q2, lambda_k2], axis=0)
    a_p = _attn(z, lam_p, subln_g, bp, sp, 0)
    a_s = _attn(z, lam_p, subln_g, bs, ss, bp * sp)

    bias_tile = jnp.repeat(b_spatial[0].T, GMLP_GROUP_DIM, axis=1)
    merged = _mix(rows, a_p, a_s, z, w_spatial[0].astype(bf), bias_tile,
                  w_branch_att[0].astype(bf), w_branch_gmlp[0].astype(bf))

    wr = jnp.concatenate([w_router_expert[0], w_router_group[0],
                          jnp.zeros((d, LANES - N_EXPERTS - N_GROUPS), jnp.float32)], axis=1)
    wr_hi = wr.astype(bf)
    wr_lo = (wr - wr_hi.astype(jnp.float32)).astype(bf)
    wr_cat = jnp.concatenate([wr_hi, wr_lo], axis=1)
    tri = jnp.triu(jnp.ones((TM_OUT, TM_OUT), bf), k=1)
    x1, h2, ri, rf, cnt = _outproj(rows, merged, xp, xs, mod3, norm2_g, w_out[0].astype(bf),
                                   wr_cat, tri)

    n_rows = (2 * rows.t // TM_EXP + N_EXPERTS) * TM_EXP
    pos_flat, meta, seg = _routing_tables(ri, cnt, n_rows // TM_EXP)
    xs_sorted = _dispatch(pos_flat, seg, h2, n_rows)
    ys = _experts(meta, xs_sorted, w_gate[0], w_up[0], w_down[0])
    y_p, y_s = _combine(rows, pos_flat, ys, x1, rf, mod3, final_norm_g.reshape(1, d))
    return y_p.reshape(bp, sp, d), y_s.reshape(bs, ss, d)
```

```python
import functools
import math

import jax
import jax.numpy as jnp
from jax import lax
from jax.experimental import pallas as pl
from jax.experimental.pallas import tpu as pltpu

D_MODEL = 2048
N_HEADS = 8
HEAD_DIM = 64
V_DIM = 2 * HEAD_DIM
QK_WIDTH = N_HEADS * 2 * HEAD_DIM
V_WIDTH = N_HEADS * V_DIM
ROPE_THETA = 10000.0
GMLP_GROUPS = 8
GMLP_GROUP_DIM = 128
GMLP_WIDTH = GMLP_GROUPS * GMLP_GROUP_DIM
GMLP_CHUNK = 128
IN_COLS = 3 * 1024 + 2 * 1024 + 2 * D_MODEL
N_GROUPS = 4
EXPERTS_PER_GROUP = 8
N_EXPERTS = N_GROUPS * EXPERTS_PER_GROUP
D_EXPERT = 512
EPS = 1e-6
LAMBDA_INIT = 0.8 - 0.6 * math.exp(-0.3 * 0)

LANES = 128
COL_TILE = 1024
N_COL_TILES = IN_COLS // COL_TILE
VMEM_LIMIT = 56 * 1024 * 1024

TM_IN = 1024
TM_MIX = 512
TM_OUT = 512
TQ = 512
TQ_SHORT = 1024
TK = 1024
TM_EXP = 256
TM_DISP = 512
TM_COMB = 256


def _cparams(sem, flags=None):
    return pltpu.CompilerParams(dimension_semantics=sem, vmem_limit_bytes=VMEM_LIMIT, flags=flags)


def _rms(x, g):
    return x * lax.rsqrt(jnp.mean(x * x, axis=-1, keepdims=True) + EPS) * g


def _pack_rows(x):
    n = x.shape[1] // 2
    hi = lax.bitcast_convert_type(x[:, :n].astype(jnp.bfloat16).astype(jnp.float32), jnp.uint32)
    lo = lax.bitcast_convert_type(x[:, n:].astype(jnp.bfloat16).astype(jnp.float32), jnp.uint32)
    return hi | (lo >> 16)


def _unpack_rows(w):
    a = lax.bitcast_convert_type(w & jnp.uint32(0xFFFF0000), jnp.float32)
    b = lax.bitcast_convert_type(w << 16, jnp.float32)
    return a, b


def _ada_kernel(c_ref, w_ref, b_ref, o_ref):
    c = c_ref[...]
    s = (c / (1.0 + jnp.exp(-c))).astype(jnp.bfloat16)
    o_ref[...] = jnp.dot(s, w_ref[...].astype(jnp.bfloat16),
                         preferred_element_type=jnp.float32) + b_ref[...]


def _ada(c8, w_ada, b_ada):
    n = w_ada.shape[1]
    tn = 1024
    return pl.pallas_call(
        _ada_kernel,
        grid=(n // tn,),
        in_specs=[pl.BlockSpec((8, D_MODEL), lambda j: (0, 0)),
                  pl.BlockSpec((D_MODEL, tn), lambda j: (0, j)),
                  pl.BlockSpec((1, tn), lambda j: (0, j))],
        out_specs=pl.BlockSpec((8, tn), lambda j: (0, j)),
        out_shape=jax.ShapeDtypeStruct((8, n), jnp.float32),
        compiler_params=_cparams(("arbitrary",)),
        name="ada",
    )(c8, w_ada, b_ada)


class _Rows:
    def __init__(self, n_prompt, n_sample_seq, sample_len):
        self.np = n_prompt
        self.ns = n_sample_seq
        self.ls = sample_len
        self.t = n_prompt + n_sample_seq * sample_len

    def seq_of_tile(self, i, tm):
        npt = self.np // tm
        return jnp.where(i < npt, 0, 1 + (i - npt) // (self.ls // tm))

    def pos_tile(self, i, tm):
        npt = self.np // tm
        return jnp.where(i < npt, i, (i - npt) % (self.ls // tm))

    def prompt_tile(self, i, tm):
        return jnp.minimum(i, self.np // tm - 1)

    def sample_tile(self, i, tm):
        return jnp.maximum(i - self.np // tm, 0)


def _inproj_kernel(rows, xp_ref, xs_ref, mod_ref, g1_ref, w_ref, cos_ref, sin_ref, gn_ref,
                   z_ref, h_ref):
    i = pl.program_id(0)
    j = pl.program_id(1)
    tm = h_ref.shape[0]

    @pl.when(j == 0)
    def _():
        m = mod_ref[0]
        x = jnp.where(i < rows.np // tm, xp_ref[...], xs_ref[...])
        h = _rms(x, g1_ref[...]) * (1.0 + m[1:2, :]) + m[0:1, :]
        h_ref[...] = h.astype(jnp.bfloat16)

    def zdot():
        return jnp.dot(h_ref[...], w_ref[...], preferred_element_type=jnp.float32)

    def rope(scale):
        z = zdot()
        c = cos_ref[...]
        s = sin_ref[...]
        lane = lax.broadcasted_iota(jnp.int32, (tm, LANES), 1)
        first_half = (lane % HEAD_DIM) < (HEAD_DIM // 2)
        for b in range(COL_TILE // LANES):
            zb = z[:, b * LANES:(b + 1) * LANES]
            up = pltpu.roll(zb, LANES - HEAD_DIM // 2, 1)
            dn = pltpu.roll(zb, HEAD_DIM // 2, 1)
            r = (zb * c + jnp.where(first_half, up, dn) * s) * scale
            z_ref[:, b * LANES:(b + 1) * LANES] = r.astype(jnp.bfloat16)

    @pl.when(j == 0)
    def _():
        rope(HEAD_DIM ** -0.5 * math.log2(math.e))

    @pl.when(j == 1)
    def _():
        rope(1.0)

    @pl.when(j == 2)
    def _():
        z_ref[...] = zdot().astype(jnp.bfloat16)

    @pl.when(j == 3)
    def _():
        z_ref[...] = jax.nn.gelu(zdot(), approximate=True).astype(jnp.bfloat16)

    @pl.when(j == 4)
    def _():
        z_ref[...] = _rms(jax.nn.gelu(zdot(), approximate=True), gn_ref[...]).astype(jnp.bfloat16)

    @pl.when(j >= 5)
    def _():
        z_ref[...] = (1.0 / (1.0 + jnp.exp(-zdot()))).astype(jnp.bfloat16)


def _inproj(rows, xp, xs, mod3, g1, w_in_bf, cos_t, sin_t, gn):
    tm = TM_IN
    return pl.pallas_call(
        functools.partial(_inproj_kernel, rows),
        grid=(rows.t // tm, N_COL_TILES),
        in_specs=[
            pl.BlockSpec((tm, D_MODEL), lambda i, j: (rows.prompt_tile(i, tm), 0),
                         pipeline_mode=pl.Buffered(1)),
            pl.BlockSpec((tm, D_MODEL), lambda i, j: (rows.sample_tile(i, tm), 0),
                         pipeline_mode=pl.Buffered(1)),
            pl.BlockSpec((1, 6, D_MODEL), lambda i, j: (rows.seq_of_tile(i, tm), 0, 0)),
            pl.BlockSpec((1, D_MODEL), lambda i, j: (0, 0)),
            pl.BlockSpec((D_MODEL, COL_TILE), lambda i, j: (0, j)),
            pl.BlockSpec((tm, LANES), lambda i, j: (rows.pos_tile(i, tm), 0)),
            pl.BlockSpec((tm, LANES), lambda i, j: (rows.pos_tile(i, tm), 0)),
            pl.BlockSpec((1, COL_TILE), lambda i, j: (0, 0)),
        ],
        out_specs=pl.BlockSpec((tm, COL_TILE), lambda i, j: (i, j)),
        out_shape=jax.ShapeDtypeStruct((rows.t, IN_COLS), jnp.bfloat16),
        scratch_shapes=[pltpu.VMEM((tm, D_MODEL), jnp.bfloat16)],
        compiler_params=_cparams(("arbitrary", "arbitrary")),
        name="inproj",
    )(xp, xs, mod3, g1, w_in_bf, cos_t, sin_t, gn)


ONES_ROWS = 16


def _attn_kernel(n_k, q_ref, k_ref, v_ref, lam_ref, sg_ref, o_ref, qq_ref, vt_ref, acc_ref, m_ref,
                 sa_ref, sb_ref):
    tq = q_ref.shape[0]
    tk = sa_ref.shape[0]

    @pl.when(pl.program_id(2) == 0)
    def _():
        vt_ref[V_DIM:, :] = jnp.ones((ONES_ROWS, vt_ref.shape[1]), jnp.bfloat16)

        def transpose_chunk(c, carry):
            off = pl.multiple_of(c * tk, tk)
            vc = v_ref[pl.ds(off, tk), :].astype(jnp.float32)
            vt_ref[0:V_DIM, pl.ds(off, tk)] = vc.T.astype(jnp.bfloat16)
            return carry

        lax.fori_loop(0, n_k, transpose_chunk, 0)

    q = q_ref[...]
    lane = lax.broadcasted_iota(jnp.int32, q.shape, 1)
    zero = jnp.zeros_like(q)
    qq_ref[0:tq, :] = jnp.where(lane < HEAD_DIM, q, zero)
    qq_ref[tq:2 * tq, :] = jnp.where(lane >= HEAD_DIM, q, zero)
    m_ref[...] = jnp.full(m_ref.shape, -jnp.inf, jnp.float32)
    acc_ref[...] = jnp.zeros(acc_ref.shape, jnp.float32)

    def scores(ik, s_ref):
        off = pl.multiple_of(ik * tk, tk)
        s_ref[...] = lax.dot_general(k_ref[pl.ds(off, tk), :], qq_ref[...], (((1,), (1,)), ((), ())),
                                     preferred_element_type=jnp.float32)

    def accumulate(ik, s_ref):
        off = pl.multiple_of(ik * tk, tk)
        s = s_ref[...]
        m_old = m_ref[...]
        m_new = jnp.maximum(m_old, jnp.max(s, axis=0, keepdims=True))
        p = jnp.exp2(s - m_new).astype(jnp.bfloat16)
        alpha = jnp.exp2(m_old - m_new)
        m_ref[...] = m_new
        pv = jnp.dot(vt_ref[:, pl.ds(off, tk)], p, preferred_element_type=jnp.float32)
        acc_ref[...] = alpha * acc_ref[...] + pv

    def chunk_pair(ik, last):
        scores(ik + 1, sb_ref)
        accumulate(ik, sa_ref)
        if not last:
            scores(ik + 2, sa_ref)
        accumulate(ik + 1, sb_ref)

    scores(0, sa_ref)

    def body(i, carry):
        chunk_pair(2 * i, False)
        return carry

    lax.fori_loop(0, n_k // 2 - 1, body, 0)
    chunk_pair(n_k - 2, True)

    lp = lam_ref[...]
    lam = (jnp.exp(jnp.sum(lp[0:1, :] * lp[1:2, :], axis=1, keepdims=True))
           - jnp.exp(jnp.sum(lp[2:3, :] * lp[3:4, :], axis=1, keepdims=True)) + LAMBDA_INIT)
    acc = acc_ref[0:V_DIM, :]
    l = acc_ref[V_DIM:V_DIM + 1, :]
    o_t =acc[:, 0:tq] / l[:, 0:tq] - lam * (acc[:, tq:2 * tq] / l[:, tq:2 * tq])
    o = o_t.T
    o_ref[...] = (_rms(o, sg_ref[...]) * (1.0 - LAMBDA_INIT)).astype(jnp.bfloat16)


def _attn(z, lam_p, subln_g, n_seq, seq_len, row0, tq):
    nq = seq_len // tq
    tk = min(TK, seq_len // 2)
    assert seq_len % (2 * tk) == 0 and row0 % seq_len == 0
    qb0 = row0 // tq
    sb0 = row0 // seq_len
    kern = functools.partial(_attn_kernel, seq_len // tk)
    return pl.pallas_call(
        kern,
        grid=(n_seq, N_HEADS, nq),
        in_specs=[
            pl.BlockSpec((tq, LANES), lambda b, h, iq: (qb0 + b * nq + iq, h)),
            pl.BlockSpec((seq_len, LANES), lambda b, h, iq: (sb0 + b, N_HEADS + h)),
            pl.BlockSpec((seq_len, LANES), lambda b, h, iq: (sb0 + b, 2 * N_HEADS + h)),
            pl.BlockSpec((4, HEAD_DIM), lambda b, h, iq: (0, 0)),
            pl.BlockSpec((1, V_DIM), lambda b, h, iq: (0, 0)),
        ],
        out_specs=pl.BlockSpec((tq, LANES), lambda b, h, iq: (b * nq + iq, h)),
        out_shape=jax.ShapeDtypeStruct((n_seq * seq_len, V_WIDTH), jnp.bfloat16),
        scratch_shapes=[pltpu.VMEM((2 * tq, LANES), jnp.bfloat16),
                        pltpu.VMEM((V_DIM + ONES_ROWS, seq_len), jnp.bfloat16),
                        pltpu.VMEM((V_DIM + ONES_ROWS, 2 * tq), jnp.float32),
                        pltpu.VMEM((1, 2 * tq), jnp.float32),
                        pltpu.VMEM((tk, 2 * tq), jnp.float32),
                        pltpu.VMEM((tk, 2 * tq), jnp.float32)],
        compiler_params=_cparams(("arbitrary", "arbitrary", "arbitrary")),
        name="attn",
    )(z, z, z, lam_p, subln_g)


def _mix_kernel(ap_ref, as_ref, gu_ref, gv_ref, sa0_ref, sa1_ref, sb0_ref, sb1_ref,
                ws_ref, bs_ref, wba_ref, wbg_ref, o_ref, m_ref, *, n_prompt_tiles):
    i = pl.program_id(0)
    tm = gu_ref.shape[0]
    bias = bs_ref[...]
    for c in range(tm // GMLP_CHUNK):
        r0 = c * GMLP_CHUNK
        for g in range(GMLP_GROUPS):
            c0 = g * GMLP_GROUP_DIM
            mixed = jnp.dot(ws_ref[g], gv_ref[r0:r0 + GMLP_CHUNK, c0:c0 + GMLP_GROUP_DIM],
                            preferred_element_type=jnp.float32) + bias[:, c0:c0 + GMLP_GROUP_DIM]
            gu = gu_ref[r0:r0 + GMLP_CHUNK, c0:c0 + GMLP_GROUP_DIM].astype(jnp.float32)
            m_ref[r0:r0 + GMLP_CHUNK, c0:c0 + GMLP_GROUP_DIM] = (gu * mixed).astype(jnp.bfloat16)

    a = jnp.where(i < n_prompt_tiles, ap_ref[...], as_ref[...])
    m = m_ref[...]
    for nb, (sa_ref, sb_ref) in enumerate(((sa0_ref, sb0_ref), (sa1_ref, sb1_ref))):
        cs = slice(nb * COL_TILE, (nb + 1) * COL_TILE)
        pa = jnp.dot(a, wba_ref[:, cs], preferred_element_type=jnp.float32)
        pb = jnp.dot(m, wbg_ref[:, cs], preferred_element_type=jnp.float32)
        merged = sa_ref[...].astype(jnp.float32) * pa + sb_ref[...].astype(jnp.float32) * pb
        o_ref[:, cs] = merged.astype(jnp.bfloat16)


def _mix(rows, a_p, a_s, z, ws_bf, bias_tile, wba_bf, wbg_bf):
    tm = TM_MIX
    zcol = lambda cb: pl.BlockSpec((tm, COL_TILE), lambda i: (i, cb))
    return pl.pallas_call(
        functools.partial(_mix_kernel, n_prompt_tiles=rows.np // tm),
        grid=(rows.t // tm,),
        in_specs=[
            pl.BlockSpec((tm, V_WIDTH), lambda i: (rows.prompt_tile(i, tm), 0)),
            pl.BlockSpec((tm, V_WIDTH), lambda i: (rows.sample_tile(i, tm), 0)),
            zcol(3), zcol(4), zcol(5), zcol(6), zcol(7), zcol(8),
            pl.BlockSpec((GMLP_GROUPS, GMLP_CHUNK, GMLP_CHUNK), lambda i: (0, 0, 0)),
            pl.BlockSpec((GMLP_CHUNK, GMLP_WIDTH), lambda i: (0, 0)),
            pl.BlockSpec((V_WIDTH, D_MODEL), lambda i: (0, 0)),
            pl.BlockSpec((GMLP_WIDTH, D_MODEL), lambda i: (0, 0)),
        ],
        out_specs=pl.BlockSpec((tm, D_MODEL), lambda i: (i, 0)),
        out_shape=jax.ShapeDtypeStruct((rows.t, D_MODEL), jnp.bfloat16),
        scratch_shapes=[pltpu.VMEM((tm, GMLP_WIDTH), jnp.bfloat16)],
        compiler_params=_cparams(("arbitrary",)),
        name="mix",
    )(a_p, a_s, z, z, z, z, z, z, ws_bf, bias_tile, wba_bf, wbg_bf)


def _outproj_kernel(rows, mg_ref, xp_ref, xs_ref, mod_ref, g2_ref, wo_ref, wr_ref, tri_ref,
                    x1_ref, h2_ref, ri_ref, rf_ref, cnt_ref, base_ref):
    i = pl.program_id(0)
    tm = mg_ref.shape[0]

    @pl.when(i == 0)
    def _():
        base_ref[...] = jnp.zeros(base_ref.shape, jnp.float32)

    m = mod_ref[0]
    y = jnp.dot(mg_ref[...], wo_ref[...], preferred_element_type=jnp.float32)

    x = jnp.where(i < rows.np // tm, xp_ref[...], xs_ref[...])
    x1 = x + m[2:3, :] * y
    x1_ref[...] = x1
    h2 = _rms(x1, g2_ref[...]) * (1.0 + m[4:5, :]) + m[3:4, :]
    h2_ref[...] = _pack_rows(h2)

    h_hi = h2.astype(jnp.bfloat16)
    h_lo = (h2 - h_hi.astype(jnp.float32)).astype(jnp.bfloat16)
    hh = jnp.dot(h_hi, wr_ref[...], preferred_element_type=jnp.float32)
    lh = jnp.dot(h_lo, wr_ref[:, 0:LANES], preferred_element_type=jnp.float32)
    lg = (hh[:, 0:LANES] + hh[:, LANES:2 * LANES] + lh).T

    row8 = lax.broadcasted_iota(jnp.int32, (8, tm), 0)
    neg = jnp.float32(-jnp.inf)
    gl = jnp.where(row8 < N_GROUPS, lg[N_EXPERTS:N_EXPERTS + 8, :], neg)
    gmax = jnp.max(gl, axis=0, keepdims=True)
    g_top = 1.0 / jnp.sum(jnp.exp(gl - gmax), axis=0, keepdims=True)
    g_idx = jnp.min(jnp.where(gl == gmax, row8, 8), axis=0, keepdims=True)
    e_sel = jnp.zeros((8, tm), jnp.float32)
    for g in range(N_GROUPS):
        e_sel = jnp.where(g_idx == g, lg[g * EXPERTS_PER_GROUP:(g + 1) * EXPERTS_PER_GROUP, :], e_sel)
    e1 = jnp.max(e_sel, axis=0, keepdims=True)
    i1 = jnp.min(jnp.where(e_sel == e1, row8, 8), axis=0, keepdims=True)
    e_rest = jnp.where(row8 == i1, neg, e_sel)
    e2 = jnp.max(e_rest, axis=0, keepdims=True)
    i2 = jnp.min(jnp.where(e_rest == e2, row8, 8), axis=0, keepdims=True)
    t = jnp.exp(e2 - e1)
    w1 = g_top / (1.0 + t)
    w2 = g_top * t / (1.0 + t)
    id1 = g_idx * EXPERTS_PER_GROUP + i1
    id2 = g_idx * EXPERTS_PER_GROUP + i2

    row32 = lax.broadcasted_iota(jnp.int32, (N_EXPERTS, tm), 0)
    oh1 = (row32 == id1).astype(jnp.float32)
    oh2 = (row32 == id2).astype(jnp.float32)
    both = oh1 + oh2
    before = jnp.dot(both.astype(jnp.bfloat16), tri_ref[...], preferred_element_type=jnp.float32)
    base = base_ref[...]
    tot = before + jnp.concatenate([base] * (tm // LANES), axis=1)
    rank1 = jnp.sum(oh1 * tot, axis=0, keepdims=True)
    rank2 = jnp.sum(oh2 * tot, axis=0, keepdims=True)
    new_base = base + jnp.sum(both, axis=1, keepdims=True)
    base_ref[...] = new_base
    cnt_ref[...] = new_base.astype(jnp.int32)

    zi = jnp.zeros((4, tm), jnp.int32)
    ri_ref[...] = jnp.concatenate([id1, id2, rank1.astype(jnp.int32), rank2.astype(jnp.int32), zi], axis=0)
    rf_ref[...] = jnp.concatenate([w1, w2, jnp.zeros((6, tm), jnp.float32)], axis=0)


def _outproj(rows, merged, xp, xs, mod3, g2, wo_bf, wr_cat, tri):
    tm = TM_OUT
    t = rows.t
    return pl.pallas_call(
        functools.partial(_outproj_kernel, rows),
        grid=(t // tm,),
        in_specs=[
            pl.BlockSpec((tm, D_MODEL), lambda i: (i, 0)),
            pl.BlockSpec((tm, D_MODEL), lambda i: (rows.prompt_tile(i, tm), 0)),
            pl.BlockSpec((tm, D_MODEL), lambda i: (rows.sample_tile(i, tm), 0)),
            pl.BlockSpec((1, 6, D_MODEL), lambda i: (rows.seq_of_tile(i, tm), 0, 0)),
            pl.BlockSpec((1, D_MODEL), lambda i: (0, 0)),
            pl.BlockSpec((D_MODEL, D_MODEL), lambda i: (0, 0), pipeline_mode=pl.Buffered(1)),
            pl.BlockSpec((D_MODEL, 2 * LANES), lambda i: (0, 0)),
            pl.BlockSpec((tm, tm), lambda i: (0, 0)),
        ],
        out_specs=[
            pl.BlockSpec((tm, D_MODEL), lambda i: (i, 0)),
            pl.BlockSpec((tm, D_MODEL // 2), lambda i: (i, 0)),
            pl.BlockSpec((8, tm), lambda i: (0, i)),
            pl.BlockSpec((8, tm), lambda i: (0, i)),
            pl.BlockSpec((N_EXPERTS, LANES), lambda i: (0, 0)),
        ],
        out_shape=[
            jax.ShapeDtypeStruct((t, D_MODEL), jnp.float32),
            jax.ShapeDtypeStruct((t, D_MODEL // 2), jnp.uint32),
            jax.ShapeDtypeStruct((8, t), jnp.int32),
            jax.ShapeDtypeStruct((8, t), jnp.float32),
            jax.ShapeDtypeStruct((N_EXPERTS, LANES), jnp.int32),
        ],
        scratch_shapes=[pltpu.VMEM((N_EXPERTS, LANES), jnp.float32)],
        compiler_params=_cparams(("arbitrary",)),
        name="outproj",
    )(merged, xp, xs, mod3, g2, wo_bf, wr_cat, tri)


def _dispatch_kernel(pos_ref, seg_ref, h2_ref, xs_ref, zero_ref, sem, zsem):
    i = pl.program_id(0)
    t_total = pl.num_programs(0) * TM_DISP
    t0 = i * TM_DISP
    n_tiles = xs_ref.shape[0] // TM_EXP

    @pl.when(i == 0)
    def _():
        zero_ref[...] = jnp.zeros(zero_ref.shape, zero_ref.dtype)

        def zero_tile(row):
            return pltpu.make_async_copy(zero_ref, xs_ref.at[pl.ds(row, TM_EXP), :], zsem)

        def for_each_zeroed_tile(action):
            for e in range(N_EXPERTS):
                @pl.when(seg_ref[1 + N_EXPERTS + e] == 1)
                def _():
                    action(pl.multiple_of(seg_ref[1 + e], TM_EXP))

            def tail(j, carry):
                @pl.when(j >= seg_ref[0])
                def _():
                    action(pl.multiple_of(j * TM_EXP, TM_EXP))
                return carry

            lax.fori_loop(0, n_tiles, tail, 0)

        for_each_zeroed_tile(lambda row: zero_tile(row).start())
        for_each_zeroed_tile(lambda row: zero_tile(row).wait())

    def row_copy(r, dst):
        return pltpu.make_async_copy(h2_ref.at[pl.ds(r, 1), :], xs_ref.at[pl.ds(dst, 1), :], sem)

    for r in range(TM_DISP):
        row_copy(r, pos_ref[t0 + r]).start(priority=0)
        row_copy(r, pos_ref[t_total + t0 + r]).start(priority=1)

    for _ in range(2 * TM_DISP):
        row_copy(0, 0).wait()


def _dispatch(pos_flat, seg, h2, n_rows):
    t, width = h2.shape
    return pl.pallas_call(
        _dispatch_kernel,
        grid_spec=pltpu.PrefetchScalarGridSpec(
            num_scalar_prefetch=2,
            grid=(t // TM_DISP,),
            in_specs=[pl.BlockSpec((TM_DISP, width), lambda i, pos, seg: (i, 0))],
            out_specs=pl.BlockSpec(memory_space=pl.ANY),
            scratch_shapes=[pltpu.VMEM((TM_EXP, width), h2.dtype),
                            pltpu.SemaphoreType.DMA, pltpu.SemaphoreType.DMA],
        ),
        out_shape=jax.ShapeDtypeStruct((n_rows, width), h2.dtype),
        compiler_params=_cparams(("arbitrary",)),
        name="dispatch",
    )(pos_flat, seg, h2)


def _experts_kernel(meta_ref, x_ref, wg_ref, wu_ref, wd_ref, y_ref, wg_s, wu_s, wd_s):
    j = pl.program_id(0)
    n_tiles = pl.num_programs(0)
    active = j < meta_ref[0]

    @pl.when(active & (meta_ref[1 + n_tiles + j] == 1))
    def _():
        wg_s[...] = wg_ref[0].astype(jnp.bfloat16)
        wu_s[...] = wu_ref[0].astype(jnp.bfloat16)
        wd_s[...] = wd_ref[0].astype(jnp.bfloat16)

    @pl.when(active)
    def _():
        xa, xb = _unpack_rows(x_ref[...])
        x = jnp.concatenate([xa.astype(jnp.bfloat16), xb.astype(jnp.bfloat16)], axis=1)
        g = jnp.dot(x, wg_s[...], preferred_element_type=jnp.float32)
        u = jnp.dot(x, wu_s[...], preferred_element_type=jnp.float32)
        hm = (g / (1.0 + jnp.exp(-g)) * u).astype(jnp.bfloat16)
        y_ref[...] = _pack_rows(jnp.dot(hm, wd_s[...], preferred_element_type=jnp.float32))

    @pl.when(jnp.logical_not(active))
    def _():
        y_ref[...] = jnp.zeros(y_ref.shape, jnp.uint32)


def _experts(meta, xs, w_gate, w_up, w_down):
    n_rows = xs.shape[0]
    n_tiles = n_rows // TM_EXP
    row = lambda j, meta: (jnp.minimum(j, meta[0] - 1), 0)
    wsel = lambda j, meta: (meta[1 + j], 0, 0)
    return pl.pallas_call(
        _experts_kernel,
        grid_spec=pltpu.PrefetchScalarGridSpec(
            num_scalar_prefetch=1,
            grid=(n_tiles,),
            in_specs=[pl.BlockSpec((TM_EXP, D_MODEL // 2), row),
                      pl.BlockSpec((1, D_MODEL, D_EXPERT), wsel),
                      pl.BlockSpec((1, D_MODEL, D_EXPERT), wsel),
                      pl.BlockSpec((1, D_EXPERT, D_MODEL), wsel)],
            out_specs=pl.BlockSpec((TM_EXP, D_MODEL // 2), lambda j, meta: (j, 0)),
            scratch_shapes=[pltpu.VMEM((D_MODEL, D_EXPERT), jnp.bfloat16),
                            pltpu.VMEM((D_MODEL, D_EXPERT), jnp.bfloat16),
                            pltpu.VMEM((D_EXPERT, D_MODEL), jnp.bfloat16)],
        ),
        out_shape=jax.ShapeDtypeStruct((n_rows, D_MODEL // 2), jnp.uint32),
        compiler_params=_cparams(("arbitrary",)),
        name="experts",
    )(meta, xs, w_gate, w_up, w_down)


COMB_SETS = 3
COMB_CHUNK = 32


def _combine_kernel(rows, pos_ref, ys_ref, x1_ref, rf_ref, mod_ref, gf_ref, op_ref, os_ref,
                    buf_ref, out_ref, sems):
    i = pl.program_id(0)
    n = pl.num_programs(0)
    tm = x1_ref.shape[0]

    def row_copy(src, st, slot, r):
        return pltpu.make_async_copy(ys_ref.at[pl.ds(src, 1), :],
                                     buf_ref.at[2 * st + slot, pl.ds(r, 1), :], sems.at[st])

    def fetch_row(tile, st, r):
        t0 = jnp.minimum(tile, n - 1) * tm
        row_copy(pos_ref[t0 + r], st, 0, r).start()
        row_copy(pos_ref[rows.t + t0 + r], st, 1, r).start()

    def drain(st):
        for _ in range(2 * tm):
            row_copy(0, st, 0, 0).wait()

    @pl.when(i == 0)
    def _():
        def first_two(r, carry):
            fetch_row(0, 0, r)
            fetch_row(1, 1, r)
            return carry

        lax.fori_loop(0, tm, first_two, 0, unroll=8)

    cur = lax.rem(i, COMB_SETS)
    nxt = lax.rem(i + 2, COMB_SETS)
    drain(cur)

    w = rf_ref[...].T
    g2 = mod_ref[0][5:6, :]
    gf = gf_ref[...]
    for c in range(tm // COMB_CHUNK):
        rs = slice(c * COMB_CHUNK, (c + 1) * COMB_CHUNK)
        y1a, y1b = _unpack_rows(buf_ref[2 * cur, rs, :])
        y2a, y2b = _unpack_rows(buf_ref[2 * cur + 1, rs, :])
        w1 = w[rs, 0:1]
        w2 = w[rs, 1:2]
        moe = jnp.concatenate([w1 * y1a + w2 * y2a, w1 * y1b + w2 * y2b], axis=1)
        out_ref[rs, :] = _rms(x1_ref[rs, :] + g2 * moe, gf)
        for r in range(c * COMB_CHUNK, (c + 1) * COMB_CHUNK):
            fetch_row(i + 2, nxt, r)

    @pl.when(i < rows.np // tm)
    def _():
        op_ref[...] = out_ref[...]

    @pl.when(i >= rows.np // tm)
    def _():
        os_ref[...] = out_ref[...]

    @pl.when(i == n - 1)
    def _():
        drain(lax.rem(i + 1, COMB_SETS))
        drain(nxt)


def _combine(rows, pos_flat, ys, x1, rf, mod3, gf):
    tm = TM_COMB
    return pl.pallas_call(
        functools.partial(_combine_kernel, rows),
        grid_spec=pltpu.PrefetchScalarGridSpec(
            num_scalar_prefetch=1,
            grid=(rows.t // tm,),
            in_specs=[
                pl.BlockSpec(memory_space=pl.ANY),
                pl.BlockSpec((tm, D_MODEL), lambda i, pos: (i, 0)),
                pl.BlockSpec((8, tm), lambda i, pos: (0, i)),
                pl.BlockSpec((1, 6, D_MODEL), lambda i, pos: (rows.seq_of_tile(i, tm), 0, 0)),
                pl.BlockSpec((1, D_MODEL), lambda i, pos: (0, 0)),
            ],
            out_specs=[
                pl.BlockSpec((tm, D_MODEL), lambda i, pos: (rows.prompt_tile(i, tm), 0)),
                pl.BlockSpec((tm, D_MODEL), lambda i, pos: (rows.sample_tile(i, tm), 0)),
            ],
            scratch_shapes=[pltpu.VMEM((2 * COMB_SETS, tm, D_MODEL // 2), jnp.uint32),
                            pltpu.VMEM((tm, D_MODEL), jnp.float32),
                            pltpu.SemaphoreType.DMA((COMB_SETS,))],
        ),
        out_shape=[jax.ShapeDtypeStruct((rows.np, D_MODEL), jnp.float32),
                   jax.ShapeDtypeStruct((rows.t - rows.np, D_MODEL), jnp.float32)],
        compiler_params=_cparams(("arbitrary",)),
        name="combine",
    )(pos_flat, ys, x1, rf, mod3, gf)


def _rope_tables(seq_len):
    pos = jnp.arange(seq_len, dtype=jnp.float32)
    inv = 1.0 / (ROPE_THETA ** (jnp.arange(0, HEAD_DIM, 2, dtype=jnp.float32) / HEAD_DIM))
    ang = pos[:, None] * inv[None, :]
    cos, sin = jnp.cos(ang), jnp.sin(ang)
    cos_t = jnp.tile(cos, (1, LANES // (HEAD_DIM // 2)))
    sin_t = jnp.tile(jnp.concatenate([-sin, sin], axis=1), (1, LANES // HEAD_DIM))
    return cos_t, sin_t


def _routing_tables(ri, cnt, n_tiles_max):
    counts = cnt[:, 0]
    padded = (counts + TM_EXP - 1) // TM_EXP * TM_EXP
    seg_end = jnp.cumsum(padded)
    seg_start = seg_end - padded
    pos1 = seg_start[ri[0]] + ri[2]
    pos2 = seg_start[ri[1]] + ri[3]
    n_active = seg_end[-1] // TM_EXP
    tile_row = jnp.minimum(jnp.arange(n_tiles_max, dtype=jnp.int32), n_active - 1) * TM_EXP
    tile_expert = jnp.sum((tile_row[:, None] >= seg_end[None, :]).astype(jnp.int32), axis=1)
    first = jnp.any(tile_row[:, None] == seg_start[None, :], axis=1) & (padded[tile_expert] > 0)
    meta = jnp.concatenate([n_active[None].astype(jnp.int32), tile_expert.astype(jnp.int32),
                            first.astype(jnp.int32)])
    seg = jnp.concatenate([n_active[None], jnp.maximum(seg_end - TM_EXP, 0),
                           (padded > 0).astype(jnp.int32)]).astype(jnp.int32)
    return jnp.concatenate([pos1, pos2]).astype(jnp.int32), meta, seg


def kernel(x_prompt, x_sample, c_prompt, c_sample, w_ada, b_ada, norm1_g, w_in, lambda_q1, lambda_k1,
           lambda_q2, lambda_k2, subln_g, gmlp_norm_g, w_spatial, b_spatial, w_branch_att,
           w_branch_gmlp, w_out, norm2_g, w_router_group, w_router_expert, w_gate, w_up, w_down,
           final_norm_g):
    bf = jnp.bfloat16
    bp, sp, d = x_prompt.shape
    bs, ss, _ = x_sample.shape
    assert bp == 1 and d == D_MODEL
    rows = _Rows(bp * sp, bs, ss)
    xp = x_prompt.reshape(bp * sp, d)
    xs = x_sample.reshape(bs * ss, d)

    c8 = jnp.concatenate([c_prompt, c_sample, jnp.zeros((8 - bp - bs, d), jnp.float32)], axis=0)
    mod3 = _ada(c8, w_ada[0], b_ada).reshape(8, 6, d)

    cos_t, sin_t = _rope_tables(max(sp, ss))
    z = _inproj(rows, xp, xs, mod3, norm1_g, w_in[0].astype(bf), cos_t, sin_t, gmlp_norm_g)

    lam_p = jnp.concatenate([lambda_q1, lambda_k1, lambda_q2, lambda_k2], axis=0)
    a_p = _attn(z, lam_p, subln_g, bp, sp, 0, TQ)
    a_s = _attn(z, lam_p, subln_g, bs, ss, bp * sp, TQ_SHORT)

    bias_tile = jnp.repeat(b_spatial[0].T, GMLP_GROUP_DIM, axis=1)
    merged = _mix(rows, a_p, a_s, z, w_spatial[0].astype(bf), bias_tile,
                  w_branch_att[0].astype(bf), w_branch_gmlp[0].astype(bf))

    wr = jnp.concatenate([w_router_expert[0], w_router_group[0],
                          jnp.zeros((d, LANES - N_EXPERTS - N_GROUPS), jnp.float32)], axis=1)
    wr_hi = wr.astype(bf)
    wr_lo = (wr - wr_hi.astype(jnp.float32)).astype(bf)
    wr_cat = jnp.concatenate([wr_hi, wr_lo], axis=1)
    tri = jnp.triu(jnp.ones((TM_OUT, TM_OUT), bf), k=1)
    x1, h2, ri, rf, cnt = _outproj(rows, merged, xp, xs, mod3, norm2_g, w_out[0].astype(bf),
                                   wr_cat, tri)

    n_rows = (2 * rows.t // TM_EXP + N_EXPERTS) * TM_EXP
    pos_flat, meta, seg = _routing_tables(ri, cnt, n_rows // TM_EXP)
    xs_sorted = _dispatch(pos_flat, seg, h2, n_rows)
    ys = _experts(meta, xs_sorted, w_gate[0], w_up[0], w_down[0])
    y_p, y_s = _combine(rows, pos_flat, ys, x1, rf, mod3, final_norm_g.reshape(1, d))
    return y_p.reshape(bp, sp, d), y_s.reshape(bs, ss, d)
```
